```python
import math
import jax
import jax.numpy as jnp
from jax import lax
import numpy as np

D_MODEL = 4096
BATCH = 4
SEQ = 2048
DEPTH = 4
DEC_BATCH = 128
DEC_SEQ = 4
PAST_LEN = 16384
PAGE_SIZE = 128

N_MIXERS = 3
N_A = (DEPTH + 2) // 3
N_B = (DEPTH + 1) // 3
N_C = DEPTH // 3

DEEPNORM_ALPHA = (2.0 * DEPTH) ** 0.25
DEEPNORM_BETA = (8.0 * DEPTH) ** -0.25
LN_EPS = 1e-5

A_HEAD = 64
A_HEADS = D_MODEL // A_HEAD
A_DECAY_LORA = max(32, int(round(1.8 * D_MODEL ** 0.5 / 32)) * 32)
A_AAA_LORA = max(32, int(round(1.8 * D_MODEL ** 0.5 / 32)) * 32)
A_MV_LORA = max(32, int(round(1.3 * D_MODEL ** 0.5 / 32)) * 32)
A_GATE_LORA = max(32, int(round(0.6 * D_MODEL ** 0.8 / 32)) * 32)
A_GN_EPS = 64e-5

B_DK = 128
B_HEADS = D_MODEL // B_DK
B_DV = 2 * B_DK
B_QK = B_HEADS * B_DK
B_V = B_HEADS * B_DV
B_CONV = 4
B_CONV_CH = 2 * B_QK + B_V
B_IN = B_CONV_CH + B_V + 2 * B_HEADS
B_CHUNK = 64

C_DK = 256
C_HEADS = D_MODEL // C_DK
C_DV = 2 * C_DK
C_QK = C_HEADS * C_DK
C_V = C_HEADS * C_DV
C_IN = 2 * C_QK + 2 * C_V
C_CHUNK = 128
ROT_BASE = 10000.0

N_EXPERTS = 32
TOP_K = 4
D_EXPERT = D_MODEL // 4
SWIGLU_ALPHA = 1.702
SWIGLU_LIMIT = 7.0
MOE_BLOCK = 64

kernel_name = 'hybrid_rwkv7_gdn_retention_moe_step'


def layer_norm(x, g, b):
    xf = x.astype(jnp.float32)
    xc = xf - xf.mean(-1, keepdims=True)
    var = jnp.mean(xc * xc, -1, keepdims=True)
    y = xc * lax.rsqrt(var + LN_EPS) * g.astype(jnp.float32) + b.astype(jnp.float32)
    return y.astype(x.dtype)


def head_norm(x, eps):
    xc = x - x.mean(-1, keepdims=True)
    return xc * lax.rsqrt(jnp.mean(xc * xc, -1, keepdims=True) + eps)


def rms_norm(x, eps):
    return x * lax.rsqrt(jnp.mean(x * x, -1, keepdims=True) + eps)


def l2_normalize(x, eps):
    return x * lax.rsqrt(jnp.sum(x * x, -1, keepdims=True) + eps)


def wkv7_scan(r, w, k, v, kk, a, s0):
    def step(s, inp):
        r_t, w_t, k_t, v_t, kk_t, a_t = inp
        sa = jnp.einsum('bhij,bhj->bhi', s, -kk_t)
        s = (s * w_t[:, :, None, :] + sa[..., None] * (kk_t * a_t)[:, :, None, :]
             + v_t[..., None] * k_t[:, :, None, :])
        return s, jnp.einsum('bhij,bhj->bhi', s, r_t)
    xs = tuple(jnp.moveaxis(t, 1, 0) for t in (r, w, k, v, kk, a))
    s, y = lax.scan(step, s0, xs)
    return jnp.moveaxis(y, 0, 1), s


def rwkv7_time_mix(x, shift0, wkv0, v_first, v_res, mu, w_rkv, w_o, w0, w1, w2,
                   a0, a1, a2, g1, g2, k_k, k_a, r_k, lnx_g, lnx_b):
    f32 = jnp.float32
    nb, t_len, d = x.shape
    x_prev = jnp.concatenate([shift0[:, None, :].astype(x.dtype), x[:, :-1]], axis=1)
    xx = x_prev - x
    mix = lambda j: x + xx * mu[j]
    r = mix(0) @ w_rkv[0]
    k = mix(1) @ w_rkv[1]
    xv = mix(2)
    v = xv @ w_rkv[2]
    w_log = -jax.nn.softplus(-(w0 + jnp.tanh(mix(3) @ w1) @ w2).astype(f32)) - 0.5
    decay = jnp.exp(-jnp.exp(w_log))
    a = jax.nn.sigmoid((a0 + (mix(4) @ a1) @ a2).astype(f32))
    g = jax.nn.sigmoid(mix(5) @ g1) @ g2
    if v_first is None:
        v_first = v
    else:
        v0, v1, v2 = v_res
        v = v + (v_first - v) * jax.nn.sigmoid(v0 + (xv @ v1) @ v2)
    hs = lambda t: t.astype(f32).reshape(nb, t_len, A_HEADS, A_HEAD)
    r, k, v, decay, a = hs(r), hs(k), hs(v), hs(decay), hs(a)
    kk = l2_normalize(k * k_k.astype(f32).reshape(A_HEADS, A_HEAD), 1e-12)
    k = k * (1.0 + (a - 1.0) * k_a.astype(f32).reshape(A_HEADS, A_HEAD))
    y, wkv = wkv7_scan(r, decay, k, v, kk, a, wkv0.astype(f32))
    y = head_norm(y, A_GN_EPS).reshape(nb, t_len, d) * lnx_g.astype(f32) + lnx_b.astype(f32)
    bonus = jnp.sum(r * k * r_k.astype(f32), -1, keepdims=True) * v
    y = (y + bonus.reshape(nb, t_len, d)) * g.astype(f32)
    return y.astype(x.dtype) @ w_o, x[:, -1], wkv, v_first


def gated_delta_chunked(q, k, v, beta, g, s0):
    nb, nh, t_len, dk = q.shape
    dv = v.shape[-1]
    c = math.gcd(t_len, B_CHUNK)
    n = t_len // c
    q, k, v = (t.reshape(nb, nh, n, c, t.shape[-1]) for t in (q, k, v))
    beta, g = (t.reshape(nb, nh, n, c) for t in (beta, g))
    gc = jnp.cumsum(g, axis=-1)
    causal = jnp.tril(jnp.ones((c, c), dtype=bool))
    eye = jnp.eye(c, dtype=jnp.float32)
    diff = gc[..., :, None] - gc[..., None, :]
    decay = jnp.where(causal, jnp.exp(jnp.where(causal, diff, 0.0)), 0.0)
    kb = k * beta[..., None]
    lower = jnp.einsum('bhncd,bhnsd->bhncs', kb, k) * decay * (1.0 - eye)
    rhs = jnp.concatenate([v * beta[..., None], kb * jnp.exp(gc)[..., None]], axis=-1)
    sol = lax.linalg.triangular_solve(lower + eye, rhs, left_side=True, lower=True)
    u, w = sol[..., :dv], sol[..., dv:]
    qk = jnp.einsum('bhncd,bhnsd->bhncs', q, k) * decay
    q_dec = q * jnp.exp(gc)[..., None]
    k_dec = k * jnp.exp(gc[..., -1:] - gc)[..., None]
    g_tot = jnp.exp(gc[..., -1])[..., None, None]

    def step(s, inp):
        u_c, w_c, qk_c, qd_c, kd_c, gt_c = inp
        v_new = u_c - jnp.einsum('bhcd,bhde->bhce', w_c, s)
        o = jnp.einsum('bhcd,bhde->bhce', qd_c, s) + jnp.einsum('bhcs,bhse->bhce', qk_c, v_new)
        s = s * gt_c + jnp.einsum('bhcd,bhce->bhde', kd_c, v_new)
        return s, o
    xs = tuple(jnp.moveaxis(t, 2, 0) for t in (u, w, qk, q_dec, k_dec, g_tot))
    s, o = lax.scan(step, s0, xs)
    return jnp.moveaxis(o, 0, 2).reshape(nb, nh, t_len, dv), s


def gated_deltanet_mix(x, conv0, s0, w_in, conv_w, a_log, dt_bias, norm_w, w_o):
    f32 = jnp.float32
    nb, t_len, _ = x.shape
    proj = x @ w_in
    qkv = proj[..., :B_CONV_CH]
    z = proj[..., B_CONV_CH:B_CONV_CH + B_V]
    a_in = proj[..., B_CONV_CH + B_V:B_CONV_CH + B_V + B_HEADS]
    b_in = proj[..., B_CONV_CH + B_V + B_HEADS:]
    xp = jnp.concatenate([conv0.astype(qkv.dtype), qkv], axis=1)
    conv = xp[:, 0:t_len] * conv_w[0]
    for j in range(1, B_CONV):
        conv = conv + xp[:, j:j + t_len] * conv_w[j]
    conv = jax.nn.silu(conv.astype(f32))
    new_conv = xp[:, t_len:]
    q = conv[..., :B_QK].reshape(nb, t_len, B_HEADS, B_DK)
    k = conv[..., B_QK:2 * B_QK].reshape(nb, t_len, B_HEADS, B_DK)
    v = conv[..., 2 * B_QK:].reshape(nb, t_len, B_HEADS, B_DV)
    q = l2_normalize(q, 1e-6) * (B_DK ** -0.5)
    k = l2_normalize(k, 1e-6)
    beta = jax.nn.sigmoid(b_in.astype(f32))
    g = -jnp.exp(a_log.astype(f32)) * jax.nn.softplus(a_in.astype(f32) + dt_bias.astype(f32))
    tr = lambda t: jnp.swapaxes(t, 1, 2)
    o, s = gated_delta_chunked(tr(q), tr(k), tr(v), tr(beta), tr(g), s0.astype(f32))
    o = rms_norm(tr(o), 1e-6) * norm_w.astype(f32)
    o = o * jax.nn.silu(z.astype(f32).reshape(nb, t_len, B_HEADS, B_DV))
    return o.reshape(nb, t_len, B_V).astype(x.dtype) @ w_o, new_conv, s


def rel_rotate(x, pos):
    half = x.shape[-1] // 2
    inv_freq = 1.0 / (ROT_BASE ** jnp.linspace(0.0, 1.0, half, dtype=jnp.float32))
    ang = pos[:, None] * inv_freq[None, :]
    sin, cos = jnp.sin(ang), jnp.cos(ang)
    x1, x2 = x[..., 0::2], x[..., 1::2]
    return jnp.stack([x1 * cos - x2 * sin, x2 * cos + x1 * sin], axis=-1).reshape(x.shape)


def retention_chunked(q, k, v, log_gamma, s0):
    nb, nh, t_len, _ = q.shape
    dv = v.shape[-1]
    c = math.gcd(t_len, C_CHUNK)
    n = t_len // c
    q, k, v = (t.reshape(nb, nh, n, c, t.shape[-1]) for t in (q, k, v))
    idx = jnp.arange(c, dtype=jnp.float32)
    lg = log_gamma[:, None]
    rel = idx[:, None] - idx[None, :]
    causal = rel >= 0
    dmat = jnp.where(causal, jnp.exp(jnp.where(causal, rel, 0.0)[None] * lg[..., None]), 0.0)
    inner = jnp.einsum('bhncd,bhnsd->bhncs', q, k) * dmat[:, None]
    o_inner = jnp.einsum('bhncs,bhnse->bhnce', inner, v)
    q_dec = q * jnp.exp((idx + 1.0) * lg)[:, None, :, None]
    k_dec = k * jnp.exp((c - 1.0 - idx) * lg)[:, None, :, None]
    chunk_decay = jnp.exp(c * log_gamma)[:, None, None]

    def step(s, inp):
        qd, kd, vc, oi = inp
        o = oi + jnp.einsum('bhcd,bhde->bhce', qd, s)
        s = s * chunk_decay + jnp.einsum('bhcd,bhce->bhde', kd, vc)
        return s, o
    xs = tuple(jnp.moveaxis(t, 2, 0) for t in (q_dec, k_dec, v, o_inner))
    s, o = lax.scan(step, s0, xs)
    return jnp.moveaxis(o, 0, 2).reshape(nb, nh, t_len, dv), s


def retention_mix(x, s0, start, w_in, w_o):
    f32 = jnp.float32
    nb, t_len, _ = x.shape
    proj = x @ w_in
    heads = lambda t, hd: jnp.swapaxes(t.astype(f32).reshape(nb, t_len, C_HEADS, hd), 1, 2)
    q = heads(proj[..., :C_QK], C_DK)
    k = heads(proj[..., C_QK:2 * C_QK], C_DK)
    v = heads(proj[..., 2 * C_QK:2 * C_QK + C_V], C_DV)
    gate = proj[..., 2 * C_QK + C_V:]
    pos = start + jnp.arange(t_len, dtype=f32)
    q = rel_rotate(q, pos)
    k = rel_rotate(k, pos) * (C_DK ** -0.5)
    log_gamma = jnp.log(1.0 - 2.0 ** (-5.0 - jnp.arange(C_HEADS, dtype=f32)))
    o, s = retention_chunked(q, k, v, log_gamma, s0.astype(f32))
    o = head_norm(jnp.swapaxes(o, 1, 2), 1e-5).reshape(nb, t_len, C_V)
    return (jax.nn.silu(gate.astype(f32)) * o).astype(x.dtype) @ w_o, s


def clamped_swiglu(h):
    glu, lin = h[..., ::2], h[..., 1::2]
    glu = jnp.minimum(glu, SWIGLU_LIMIT)
    lin = jnp.clip(lin, -SWIGLU_LIMIT, SWIGLU_LIMIT)
    return glu * jax.nn.sigmoid(SWIGLU_ALPHA * glu) * (lin + 1.0)


def routed_moe(x, w_router, b_router, w1, b1, w2, b2):
    shp = x.shape
    xt = x.reshape(-1, shp[-1])
    n_tok = xt.shape[0]
    logits = (xt @ w_router + b_router).astype(jnp.float32)
    top_val, top_idx = lax.top_k(logits, TOP_K)
    gates = jax.nn.softmax(top_val, axis=-1)
    tk = n_tok * TOP_K
    flat_e = top_idx.reshape(tk)
    order = jnp.argsort(flat_e)
    sorted_e = flat_e[order]
    counts = jnp.bincount(flat_e, length=N_EXPERTS)
    padded = (counts + MOE_BLOCK - 1) // MOE_BLOCK * MOE_BLOCK
    start = jnp.cumsum(counts) - counts
    pad_end = jnp.cumsum(padded)
    pad_start = pad_end - padded
    dest_sorted = pad_start[sorted_e] + jnp.arange(tk, dtype=jnp.int32) - start[sorted_e]
    dest = jnp.zeros((tk,), jnp.int32).at[order].set(dest_sorted.astype(jnp.int32))
    n_blocks = -(-(tk + N_EXPERTS * (MOE_BLOCK - 1)) // MOE_BLOCK)
    n_rows = n_blocks * MOE_BLOCK
    row_tok = jnp.full((n_rows,), n_tok, jnp.int32).at[dest].set(
        jnp.arange(tk, dtype=jnp.int32) // TOP_K)
    block_e = jnp.minimum(
        jnp.searchsorted(pad_end, jnp.arange(n_blocks, dtype=jnp.int32) * MOE_BLOCK, side='right'),
        N_EXPERTS - 1)
    x_ext = jnp.concatenate([xt, jnp.zeros((1, xt.shape[-1]), xt.dtype)], axis=0)
    x_rows = x_ext[row_tok].reshape(n_blocks, MOE_BLOCK, xt.shape[-1])

    def expert_block(args):
        xb, e = args
        h = clamped_swiglu((xb @ w1[e] + b1[e]).astype(jnp.float32))
        return h.astype(xb.dtype) @ w2[e] + b2[e]
    y_rows = lax.map(expert_block, (x_rows, block_e)).reshape(n_rows, xt.shape[-1])
    y_sel = y_rows[dest].reshape(n_tok, TOP_K, xt.shape[-1])
    y = jnp.einsum('tkd,tk->td', y_sel, gates.astype(y_sel.dtype))
    return y.reshape(shp)


def setup_inputs(seed: int = 0) -> dict:
    key = jax.random.key(seed)
    ks = iter(jax.random.split(key, 64))
    f32 = jnp.float32
    D = D_MODEL

    def nrm(shape, scale):
        return jax.random.normal(next(ks), shape, f32) * scale

    def uni(shape, lo, hi):
        return jax.random.uniform(next(ks), shape, f32, lo, hi)

    dt = jnp.exp(uni((N_B, B_HEADS), math.log(1e-3), math.log(1e-1)))
    gdn_dt_bias = dt + jnp.log(-jnp.expm1(-dt))
    return {
        'x_prompt': nrm((BATCH, SEQ, D), 1.0),
        'x_sample': nrm((DEC_BATCH, DEC_SEQ, D), 1.0),
        'state_rwkv_shift': nrm((N_A, DEC_BATCH, D), 1.0),
        'state_rwkv_wkv': nrm((N_A, DEC_BATCH, A_HEADS, A_HEAD, A_HEAD), 0.1),
        'state_gdn_conv': nrm((N_B, DEC_BATCH, B_CONV - 1, B_CONV_CH), 1.0),
        'state_gdn_delta': nrm((N_B, DEC_BATCH, B_HEADS, B_DK, B_DV), 0.1),
        'state_ret': nrm((N_C, DEC_BATCH, C_HEADS, C_DK, C_DV), 0.1),
        'ln_gain': 1.0 + nrm((DEPTH, 2, D), 0.02),
        'ln_bias': nrm((DEPTH, 2, D), 0.02),
        'rwkv_mu': uni((N_A, 6, D), 0.0, 1.0),
        'rwkv_w_rkv': nrm((N_A, 3, D, D), D ** -0.5),
        'rwkv_w_o': nrm((N_A, D, D), D ** -0.5 * DEEPNORM_BETA),
        'rwkv_w0': uni((N_A, D), -4.0, 1.0),
        'rwkv_w1': nrm((N_A, D, A_DECAY_LORA), D ** -0.5),
        'rwkv_w2': nrm((N_A, A_DECAY_LORA, D), 0.1 * A_DECAY_LORA ** -0.5),
        'rwkv_a0': nrm((N_A, D), 0.1),
        'rwkv_a1': nrm((N_A, D, A_AAA_LORA), D ** -0.5),
        'rwkv_a2': nrm((N_A, A_AAA_LORA, D), 0.1 * A_AAA_LORA ** -0.5),
        'rwkv_g1': nrm((N_A, D, A_GATE_LORA), D ** -0.5),
        'rwkv_g2': nrm((N_A, A_GATE_LORA, D), A_GATE_LORA ** -0.5),
        'rwkv_k_k': 0.85 + nrm((N_A, D), 0.02),
        'rwkv_k_a': 1.0 + nrm((N_A, D), 0.02),
        'rwkv_r_k': nrm((N_A, A_HEADS, A_HEAD), 0.1),
        'rwkv_lnx_g': 1.0 + nrm((N_A, D), 0.02),
        'rwkv_lnx_b': nrm((N_A, D), 0.02),
        'rwkv_v0': 1.0 + nrm((N_A - 1, D), 0.1),
        'rwkv_v1': nrm((N_A - 1, D, A_MV_LORA), D ** -0.5),
        'rwkv_v2': nrm((N_A - 1, A_MV_LORA, D), 0.1 * A_MV_LORA ** -0.5),
        'gdn_w_in': nrm((N_B, D, B_IN), D ** -0.5),
        'gdn_conv_w': nrm((N_B, B_CONV, B_CONV_CH), B_CONV ** -0.5),
        'gdn_a_log': jnp.log(uni((N_B, B_HEADS), 1.0, 16.0)),
        'gdn_dt_bias': gdn_dt_bias,
        'gdn_norm_w': 1.0 + nrm((N_B, B_DV), 0.02),
        'gdn_w_o': nrm((N_B, B_V, D), B_V ** -0.5 * DEEPNORM_BETA),
        'ret_w_in': nrm((N_C, D, C_IN), D ** -0.5),
        'ret_w_o': nrm((N_C, C_V, D), C_V ** -0.5 * DEEPNORM_BETA),
        'moe_w_router': nrm((DEPTH, D, N_EXPERTS), D ** -0.5),
        'moe_b_router': nrm((DEPTH, N_EXPERTS), 0.01),
        'moe_w1': nrm((DEPTH, N_EXPERTS, D, 2 * D_EXPERT), D ** -0.5),
        'moe_b1': nrm((DEPTH, N_EXPERTS, 2 * D_EXPERT), 0.01),
        'moe_w2': nrm((DEPTH, N_EXPERTS, D_EXPERT, D), D_EXPERT ** -0.5 * DEEPNORM_BETA),
        'moe_b2': nrm((DEPTH, N_EXPERTS, D), 0.01),
    }


def reference(x_prompt, x_sample, state_rwkv_shift, state_rwkv_wkv, state_gdn_conv,
              state_gdn_delta, state_ret, ln_gain, ln_bias,
              rwkv_mu, rwkv_w_rkv, rwkv_w_o, rwkv_w0, rwkv_w1, rwkv_w2,
              rwkv_a0, rwkv_a1, rwkv_a2, rwkv_g1, rwkv_g2, rwkv_k_k, rwkv_k_a,
              rwkv_r_k, rwkv_lnx_g, rwkv_lnx_b, rwkv_v0, rwkv_v1, rwkv_v2,
              gdn_w_in, gdn_conv_w, gdn_a_log, gdn_dt_bias, gdn_norm_w, gdn_w_o,
              ret_w_in, ret_w_o,
              moe_w_router, moe_b_router, moe_w1, moe_b1, moe_w2, moe_b2):
    f32 = jnp.float32
    n_p = x_prompt.shape[0]
    hp, hs = x_prompt, x_sample
    vfirst_p = None
    vfirst_s = None
    p_shift, p_wkv, p_conv, p_delta, p_ret = [], [], [], [], []
    s_shift, s_wkv, s_conv, s_delta, s_ret = [], [], [], [], []
    for i in range(DEPTH):
        kind, j = i % N_MIXERS, i // N_MIXERS
        if kind == 0:
            prm = (rwkv_mu[j], rwkv_w_rkv[j], rwkv_w_o[j], rwkv_w0[j], rwkv_w1[j], rwkv_w2[j],
                   rwkv_a0[j], rwkv_a1[j], rwkv_a2[j], rwkv_g1[j], rwkv_g2[j], rwkv_k_k[j],
                   rwkv_k_a[j], rwkv_r_k[j], rwkv_lnx_g[j], rwkv_lnx_b[j])
            v_res = None if j == 0 else (rwkv_v0[j - 1], rwkv_v1[j - 1], rwkv_v2[j - 1])
            mp, sh_p, st_p, vfirst_p = rwkv7_time_mix(
                hp, jnp.zeros((n_p, D_MODEL), hp.dtype),
                jnp.zeros((n_p, A_HEADS, A_HEAD, A_HEAD), f32), vfirst_p, v_res, *prm)
            ms, sh_s, st_s, vfirst_s = rwkv7_time_mix(
                hs, state_rwkv_shift[j], state_rwkv_wkv[j], vfirst_s, v_res, *prm)
            p_shift.append(sh_p)
            p_wkv.append(st_p)
            s_shift.append(sh_s)
            s_wkv.append(st_s)
        elif kind == 1:
            prm = (gdn_w_in[j], gdn_conv_w[j], gdn_a_log[j], gdn_dt_bias[j], gdn_norm_w[j], gdn_w_o[j])
            mp, cv_p, st_p = gated_deltanet_mix(
                hp, jnp.zeros((n_p, B_CONV - 1, B_CONV_CH), hp.dtype),
                jnp.zeros((n_p, B_HEADS, B_DK, B_DV), f32), *prm)
            ms, cv_s, st_s = gated_deltanet_mix(hs, state_gdn_conv[j], state_gdn_delta[j], *prm)
            p_conv.append(cv_p)
            p_delta.append(st_p)
            s_conv.append(cv_s)
            s_delta.append(st_s)
        else:
            mp, st_p = retention_mix(hp, jnp.zeros((n_p, C_HEADS, C_DK, C_DV), f32), 0,
                                     ret_w_in[j], ret_w_o[j])
            ms, st_s = retention_mix(hs, state_ret[j], PAST_LEN, ret_w_in[j], ret_w_o[j])
            p_ret.append(st_p)
            s_ret.append(st_s)
        hp = layer_norm(DEEPNORM_ALPHA * hp + mp, ln_gain[i, 0], ln_bias[i, 0])
        hs = layer_norm(DEEPNORM_ALPHA * hs + ms, ln_gain[i, 0], ln_bias[i, 0])
        moe_prm = (moe_w_router[i], moe_b_router[i], moe_w1[i], moe_b1[i], moe_w2[i], moe_b2[i])
        hp = layer_norm(DEEPNORM_ALPHA * hp + routed_moe(hp, *moe_prm), ln_gain[i, 1], ln_bias[i, 1])
        hs = layer_norm(DEEPNORM_ALPHA * hs + routed_moe(hs, *moe_prm), ln_gain[i, 1], ln_bias[i, 1])
    stk = lambda rows, like: jnp.stack(rows).astype(like.dtype)
    return (hp, hs,
            stk(p_shift, state_rwkv_shift), stk(p_wkv, state_rwkv_wkv),
            stk(p_conv, state_gdn_conv), stk(p_delta, state_gdn_delta), stk(p_ret, state_ret),
            stk(s_shift, state_rwkv_shift), stk(s_wkv, state_rwkv_wkv),
            stk(s_conv, state_gdn_conv), stk(s_delta, state_gdn_delta), stk(s_ret, state_ret))
```

```python
import functools
import math

import jax
import jax.numpy as jnp
from jax import lax
from jax.experimental import pallas as pl
from jax.experimental.pallas import tpu as pltpu

F32 = jnp.float32
BF16 = jnp.bfloat16

D_MODEL = 4096
DEPTH = 4
PAST_LEN = 16384
N_MIXERS = 3

DEEPNORM_ALPHA = (2.0 * DEPTH) ** 0.25
LN_EPS = 1e-5

A_HEAD = 64
A_HEADS = D_MODEL // A_HEAD
A_GN_EPS = 64e-5

B_DK = 128
B_HEADS = D_MODEL // B_DK
B_DV = 2 * B_DK
B_QK = B_HEADS * B_DK
B_V = B_HEADS * B_DV
B_CONV = 4
B_CONV_CH = 2 * B_QK + B_V
B_CHUNK = 64

C_DK = 256
C_HEADS = D_MODEL // C_DK
C_DV = 2 * C_DK
C_QK = C_HEADS * C_DK
C_V = C_HEADS * C_DV
C_CHUNK = 128
ROT_BASE = 10000.0

N_EXPERTS = 32
TOP_K = 4
D_EXPERT = D_MODEL // 4
SWIGLU_ALPHA = 1.702
SWIGLU_LIMIT = 7.0

MOE_ROWS = 256
MOE_TN1 = 1024
MOE_TN2 = 2048
LANE_PAIR_GROUP = 256
VMEM_LIMIT = 56 * 1024 * 1024


def layer_norm(x, g, b):
    xf = x.astype(F32)
    xc = xf - xf.mean(-1, keepdims=True)
    var = jnp.mean(xc * xc, -1, keepdims=True)
    y = xc * lax.rsqrt(var + LN_EPS) * g.astype(F32) + b.astype(F32)
    return y.astype(x.dtype)


def head_norm(x, eps):
    xc = x - x.mean(-1, keepdims=True)
    return xc * lax.rsqrt(jnp.mean(xc * xc, -1, keepdims=True) + eps)


def rms_norm(x, eps):
    return x * lax.rsqrt(jnp.mean(x * x, -1, keepdims=True) + eps)


def l2_normalize(x, eps):
    return x * lax.rsqrt(jnp.sum(x * x, -1, keepdims=True) + eps)


def wkv7_scan(r, w, k, v, kk, a, s0):
    def step(s, inp):
        r_t, w_t, k_t, v_t, kk_t, a_t = inp
        sa = jnp.einsum('bhij,bhj->bhi', s, -kk_t)
        s = (s * w_t[:, :, None, :] + sa[..., None] * (kk_t * a_t)[:, :, None, :]
             + v_t[..., None] * k_t[:, :, None, :])
        return s, jnp.einsum('bhij,bhj->bhi', s, r_t)
    xs = tuple(jnp.moveaxis(t, 1, 0) for t in (r, w, k, v, kk, a))
    s, y = lax.scan(step, s0, xs)
    return jnp.moveaxis(y, 0, 1), s


def rwkv7_time_mix(x, shift0, wkv0, v_first, v_res, mu, w_rkv, w_o, w0, w1, w2,
                   a0, a1, a2, g1, g2, k_k, k_a, r_k, lnx_g, lnx_b):
    nb, t_len, d = x.shape
    x_prev = jnp.concatenate([shift0[:, None, :].astype(x.dtype), x[:, :-1]], axis=1)
    xx = x_prev - x
    mix = lambda j: x + xx * mu[j]
    r = mix(0) @ w_rkv[0]
    k = mix(1) @ w_rkv[1]
    xv = mix(2)
    v = xv @ w_rkv[2]
    w_log = -jax.nn.softplus(-(w0 + jnp.tanh(mix(3) @ w1) @ w2).astype(F32)) - 0.5
    decay = jnp.exp(-jnp.exp(w_log))
    a = jax.nn.sigmoid((a0 + (mix(4) @ a1) @ a2).astype(F32))
    g = jax.nn.sigmoid(mix(5) @ g1) @ g2
    if v_first is None:
        v_first = v
    else:
        v0, v1, v2 = v_res
        v = v + (v_first - v) * jax.nn.sigmoid(v0 + (xv @ v1) @ v2)
    hs = lambda t: t.astype(F32).reshape(nb, t_len, A_HEADS, A_HEAD)
    r, k, v, decay, a = hs(r), hs(k), hs(v), hs(decay), hs(a)
    kk = l2_normalize(k * k_k.astype(F32).reshape(A_HEADS, A_HEAD), 1e-12)
    k = k * (1.0 + (a - 1.0) * k_a.astype(F32).reshape(A_HEADS, A_HEAD))
    y, wkv = wkv7_scan(r, decay, k, v, kk, a, wkv0.astype(F32))
    y = head_norm(y, A_GN_EPS).reshape(nb, t_len, d) * lnx_g.astype(F32) + lnx_b.astype(F32)
    bonus = jnp.sum(r * k * r_k.astype(F32), -1, keepdims=True) * v
    y = (y + bonus.reshape(nb, t_len, d)) * g.astype(F32)
    return y.astype(x.dtype) @ w_o, x[:, -1], wkv, v_first


def gated_delta_chunked(q, k, v, beta, g, s0):
    nb, nh, t_len, dk = q.shape
    dv = v.shape[-1]
    c = math.gcd(t_len, B_CHUNK)
    n = t_len // c
    q, k, v = (t.reshape(nb, nh, n, c, t.shape[-1]) for t in (q, k, v))
    beta, g = (t.reshape(nb, nh, n, c) for t in (beta, g))
    gc = jnp.cumsum(g, axis=-1)
    causal = jnp.tril(jnp.ones((c, c), dtype=bool))
    eye = jnp.eye(c, dtype=jnp.float32)
    diff = gc[..., :, None] - gc[..., None, :]
    decay = jnp.where(causal, jnp.exp(jnp.where(causal, diff, 0.0)), 0.0)
    kb = k * beta[..., None]
    lower = jnp.einsum('bhncd,bhnsd->bhncs', kb, k) * decay * (1.0 - eye)
    rhs = jnp.concatenate([v * beta[..., None], kb * jnp.exp(gc)[..., None]], axis=-1)
    sol = lax.linalg.triangular_solve(lower + eye, rhs, left_side=True, lower=True)
    u, w = sol[..., :dv], sol[..., dv:]
    qk = jnp.einsum('bhncd,bhnsd->bhncs', q, k) * decay
    q_dec = q * jnp.exp(gc)[..., None]
    k_dec = k * jnp.exp(gc[..., -1:] - gc)[..., None]
    g_tot = jnp.exp(gc[..., -1])[..., None, None]

    def step(s, inp):
        u_c, w_c, qk_c, qd_c, kd_c, gt_c = inp
        v_new = u_c - jnp.einsum('bhcd,bhde->bhce', w_c, s)
        o = jnp.einsum('bhcd,bhde->bhce', qd_c, s) + jnp.einsum('bhcs,bhse->bhce', qk_c, v_new)
        s = s * gt_c + jnp.einsum('bhcd,bhce->bhde', kd_c, v_new)
        return s, o
    xs = tuple(jnp.moveaxis(t, 2, 0) for t in (u, w, qk, q_dec, k_dec, g_tot))
    s, o = lax.scan(step, s0, xs)
    return jnp.moveaxis(o, 0, 2).reshape(nb, nh, t_len, dv), s


def gated_deltanet_mix(x, conv0, s0, w_in, conv_w, a_log, dt_bias, norm_w, w_o):
    nb, t_len, _ = x.shape
    proj = x @ w_in
    qkv = proj[..., :B_CONV_CH]
    z = proj[..., B_CONV_CH:B_CONV_CH + B_V]
    a_in = proj[..., B_CONV_CH + B_V:B_CONV_CH + B_V + B_HEADS]
    b_in = proj[..., B_CONV_CH + B_V + B_HEADS:]
    xp = jnp.concatenate([conv0.astype(qkv.dtype), qkv], axis=1)
    conv = xp[:, 0:t_len] * conv_w[0]
    for j in range(1, B_CONV):
        conv = conv + xp[:, j:j + t_len] * conv_w[j]
    conv = jax.nn.silu(conv.astype(F32))
    new_conv = xp[:, t_len:]
    q = conv[..., :B_QK].reshape(nb, t_len, B_HEADS, B_DK)
    k = conv[..., B_QK:2 * B_QK].reshape(nb, t_len, B_HEADS, B_DK)
    v = conv[..., 2 * B_QK:].reshape(nb, t_len, B_HEADS, B_DV)
    q = l2_normalize(q, 1e-6) * (B_DK ** -0.5)
    k = l2_normalize(k, 1e-6)
    beta = jax.nn.sigmoid(b_in.astype(F32))
    g = -jnp.exp(a_log.astype(F32)) * jax.nn.softplus(a_in.astype(F32) + dt_bias.astype(F32))
    tr = lambda t: jnp.swapaxes(t, 1, 2)
    o, s = gated_delta_chunked(tr(q), tr(k), tr(v), tr(beta), tr(g), s0.astype(F32))
    o = rms_norm(tr(o), 1e-6) * norm_w.astype(F32)
    o = o * jax.nn.silu(z.astype(F32).reshape(nb, t_len, B_HEADS, B_DV))
    return o.reshape(nb, t_len, B_V).astype(x.dtype) @ w_o, new_conv, s


def rel_rotate(x, pos):
    half = x.shape[-1] // 2
    inv_freq = 1.0 / (ROT_BASE ** jnp.linspace(0.0, 1.0, half, dtype=jnp.float32))
    ang = pos[:, None] * inv_freq[None, :]
    sin, cos = jnp.sin(ang), jnp.cos(ang)
    x1, x2 = x[..., 0::2], x[..., 1::2]
    return jnp.stack([x1 * cos - x2 * sin, x2 * cos + x1 * sin], axis=-1).reshape(x.shape)


def retention_chunked(q, k, v, log_gamma, s0):
    nb, nh, t_len, _ = q.shape
    dv = v.shape[-1]
    c = math.gcd(t_len, C_CHUNK)
    n = t_len // c
    q, k, v = (t.reshape(nb, nh, n, c, t.shape[-1]) for t in (q, k, v))
    idx = jnp.arange(c, dtype=jnp.float32)
    lg = log_gamma[:, None]
    rel = idx[:, None] - idx[None, :]
    causal = rel >= 0
    dmat = jnp.where(causal, jnp.exp(jnp.where(causal, rel, 0.0)[None] * lg[..., None]), 0.0)
    inner = jnp.einsum('bhncd,bhnsd->bhncs', q, k) * dmat[:, None]
    o_inner = jnp.einsum('bhncs,bhnse->bhnce', inner, v)
    q_dec = q * jnp.exp((idx + 1.0) * lg)[:, None, :, None]
    k_dec = k * jnp.exp((c - 1.0 - idx) * lg)[:, None, :, None]
    chunk_decay = jnp.exp(c * log_gamma)[:, None, None]

    def step(s, inp):
        qd, kd, vc, oi = inp
        o = oi + jnp.einsum('bhcd,bhde->bhce', qd, s)
        s = s * chunk_decay + jnp.einsum('bhcd,bhce->bhde', kd, vc)
        return s, o
    xs = tuple(jnp.moveaxis(t, 2, 0) for t in (q_dec, k_dec, v, o_inner))
    s, o = lax.scan(step, s0, xs)
    return jnp.moveaxis(o, 0, 2).reshape(nb, nh, t_len, dv), s


def retention_mix(x, s0, start, w_in, w_o):
    nb, t_len, _ = x.shape
    proj = x @ w_in
    heads = lambda t, hd: jnp.swapaxes(t.astype(F32).reshape(nb, t_len, C_HEADS, hd), 1, 2)
    q = heads(proj[..., :C_QK], C_DK)
    k = heads(proj[..., C_QK:2 * C_QK], C_DK)
    v = heads(proj[..., 2 * C_QK:2 * C_QK + C_V], C_DV)
    gate = proj[..., 2 * C_QK + C_V:]
    pos = start + jnp.arange(t_len, dtype=F32)
    q = rel_rotate(q, pos)
    k = rel_rotate(k, pos) * (C_DK ** -0.5)
    log_gamma = jnp.log(1.0 - 2.0 ** (-5.0 - jnp.arange(C_HEADS, dtype=F32)))
    o, s = retention_chunked(q, k, v, log_gamma, s0.astype(F32))
    o = head_norm(jnp.swapaxes(o, 1, 2), 1e-5).reshape(nb, t_len, C_V)
    return (jax.nn.silu(gate.astype(F32)) * o).astype(x.dtype) @ w_o, s


def _moe_up_kernel(blk_e_ref, blk_first_ref, n_used_ref, x_ref, w1_ref, b1_ref, sel_ref,
                   o_ref, wbf_ref):
    del blk_e_ref
    i = pl.program_id(1)
    tn = w1_ref.shape[-1]

    @pl.when(i >= n_used_ref[0])
    def _():
        o_ref[...] = jnp.zeros_like(o_ref)

    @pl.when(i < n_used_ref[0])
    def _():
        @pl.when(blk_first_ref[i] == 1)
        def _():
            wbf_ref[...] = w1_ref[...].astype(BF16)

        h = jnp.dot(x_ref[...], wbf_ref[...], preferred_element_type=F32) + b1_ref[...]
        glu = jnp.minimum(h, SWIGLU_LIMIT)
        lin = jnp.clip(pltpu.roll(h, tn - 1, axis=1), -SWIGLU_LIMIT, SWIGLU_LIMIT)
        act = (glu * jax.nn.sigmoid(SWIGLU_ALPHA * glu) * (lin + 1.0)).astype(BF16)
        half = LANE_PAIR_GROUP // 2
        for c in range(tn // LANE_PAIR_GROUP):
            grp = act[:, c * LANE_PAIR_GROUP:(c + 1) * LANE_PAIR_GROUP]
            o_ref[:, c * half:(c + 1) * half] = jnp.dot(
                grp, sel_ref[...], preferred_element_type=F32).astype(o_ref.dtype)


def _moe_down_kernel(blk_e_ref, blk_first_ref, n_used_ref, h_ref, w2_ref, b2_ref, o_ref, wbf_ref):
    del blk_e_ref
    i = pl.program_id(1)

    @pl.when(i >= n_used_ref[0])
    def _():
        o_ref[...] = jnp.zeros_like(o_ref)

    @pl.when(i < n_used_ref[0])
    def _():
        @pl.when(blk_first_ref[i] == 1)
        def _():
            wbf_ref[...] = w2_ref[...].astype(BF16)

        o_ref[...] = jnp.dot(h_ref[...], wbf_ref[...], preferred_element_type=F32) + b2_ref[...]


def _moe_experts(x_rows, blk_e, blk_first, n_used, w1, b1, w2, b2, layer):
    n_rows = x_rows.shape[0]
    n_blocks = n_rows // MOE_ROWS
    d_up = 2 * D_EXPERT
    sel = (jnp.arange(LANE_PAIR_GROUP)[:, None] == 2 * jnp.arange(LANE_PAIR_GROUP // 2)[None, :]).astype(BF16)
    b1r = b1.reshape(DEPTH, N_EXPERTS, 1, d_up)
    b2r = b2.reshape(DEPTH, N_EXPERTS, 1, D_MODEL)
    row_blk = lambda j, i, be, bf, nu: (jnp.minimum(i, nu[0] - 1), 0)
    params = pltpu.CompilerParams(dimension_semantics=("arbitrary", "arbitrary"),
                                  vmem_limit_bytes=VMEM_LIMIT)

    act = pl.pallas_call(
        _moe_up_kernel,
        grid_spec=pltpu.PrefetchScalarGridSpec(
            num_scalar_prefetch=3,
            grid=(d_up // MOE_TN1, n_blocks),
            in_specs=[
                pl.BlockSpec((MOE_ROWS, D_MODEL), row_blk),
                pl.BlockSpec((None, None, D_MODEL, MOE_TN1),
                             lambda j, i, be, bf, nu: (layer, be[i], 0, j)),
                pl.BlockSpec((None, None, 1, MOE_TN1),
                             lambda j, i, be, bf, nu: (layer, be[i], 0, j)),
                pl.BlockSpec((LANE_PAIR_GROUP, LANE_PAIR_GROUP // 2),
                             lambda j, i, be, bf, nu: (0, 0)),
            ],
            out_specs=pl.BlockSpec((MOE_ROWS, MOE_TN1 // 2), lambda j, i, be, bf, nu: (i, j)),
            scratch_shapes=[pltpu.VMEM((D_MODEL, MOE_TN1), BF16)],
        ),
        out_shape=jax.ShapeDtypeStruct((n_rows, D_EXPERT), BF16),
        compiler_params=params,
        name="moe_up",
    )(blk_e, blk_first, n_used, x_rows, w1, b1r, sel)

    return pl.pallas_call(
        _moe_down_kernel,
        grid_spec=pltpu.PrefetchScalarGridSpec(
            num_scalar_prefetch=3,
            grid=(D_MODEL // MOE_TN2, n_blocks),
            in_specs=[
                pl.BlockSpec((MOE_ROWS, D_EXPERT), row_blk),
                pl.BlockSpec((None, None, D_EXPERT, MOE_TN2),
                             lambda j, i, be, bf, nu: (layer, be[i], 0, j)),
                pl.BlockSpec((None, None, 1, MOE_TN2),
                             lambda j, i, be, bf, nu: (layer, be[i], 0, j)),
            ],
            out_specs=pl.BlockSpec((MOE_ROWS, MOE_TN2), lambda j, i, be, bf, nu: (i, j)),
            scratch_shapes=[pltpu.VMEM((D_EXPERT, MOE_TN2), BF16)],
        ),
        out_shape=jax.ShapeDtypeStruct((n_rows, D_MODEL), F32),
        compiler_params=params,
        name="moe_down",
    )(blk_e, blk_first, n_used, act, w2, b2r)


def routed_moe(xt, w_router, b_router, w1, b1, w2, b2, layer):
    n_tok = xt.shape[0]
    logits = (xt @ w_router[layer] + b_router[layer]).astype(F32)
    top_val, top_idx = lax.top_k(logits, TOP_K)
    gates = jax.nn.softmax(top_val, axis=-1)
    tk = n_tok * TOP_K
    flat_e = top_idx.reshape(tk)
    order = jnp.argsort(flat_e)
    sorted_e = flat_e[order]
    counts = jnp.bincount(flat_e, length=N_EXPERTS)
    padded = (counts + MOE_ROWS - 1) // MOE_ROWS * MOE_ROWS
    start = jnp.cumsum(counts) - counts
    pad_end = jnp.cumsum(padded)
    pad_start = pad_end - padded
    dest_sorted = pad_start[sorted_e] + jnp.arange(tk, dtype=jnp.int32) - start[sorted_e]
    dest = jnp.zeros((tk,), jnp.int32).at[order].set(dest_sorted.astype(jnp.int32))
    n_blocks = -(-(tk + N_EXPERTS * (MOE_ROWS - 1)) // MOE_ROWS)
    n_rows = n_blocks * MOE_ROWS
    row_tok = jnp.full((n_rows,), n_tok, jnp.int32).at[dest].set(
        jnp.arange(tk, dtype=jnp.int32) // TOP_K)
    blk_e = jnp.minimum(
        jnp.searchsorted(pad_end, jnp.arange(n_blocks, dtype=jnp.int32) * MOE_ROWS, side='right'),
        N_EXPERTS - 1).astype(jnp.int32)
    blk_first = jnp.concatenate(
        [jnp.ones((1,), jnp.int32), (blk_e[1:] != blk_e[:-1]).astype(jnp.int32)])
    n_used = (pad_end[-1:] // MOE_ROWS).astype(jnp.int32)
    x_ext = jnp.concatenate([xt.astype(BF16), jnp.zeros((1, xt.shape[-1]), BF16)], axis=0)
    x_rows = x_ext[row_tok]
    y_rows = _moe_experts(x_rows, blk_e, blk_first, n_used, w1, b1, w2, b2, layer)
    y_sel = y_rows[dest].reshape(n_tok, TOP_K, xt.shape[-1])
    return jnp.einsum('tkd,tk->td', y_sel, gates.astype(y_sel.dtype),
                      precision=lax.Precision.HIGHEST)


def kernel(x_prompt, x_sample, state_rwkv_shift, state_rwkv_wkv, state_gdn_conv, state_gdn_delta, state_ret, ln_gain, ln_bias, rwkv_mu, rwkv_w_rkv, rwkv_w_o, rwkv_w0, rwkv_w1, rwkv_w2, rwkv_a0, rwkv_a1, rwkv_a2, rwkv_g1, rwkv_g2, rwkv_k_k, rwkv_k_a, rwkv_r_k, rwkv_lnx_g, rwkv_lnx_b, rwkv_v0, rwkv_v1, rwkv_v2, gdn_w_in, gdn_conv_w, gdn_a_log, gdn_dt_bias, gdn_norm_w, gdn_w_o, ret_w_in, ret_w_o, moe_w_router, moe_b_router, moe_w1, moe_b1, moe_w2, moe_b2):
    n_p = x_prompt.shape[0]
    n_ptok = x_prompt.shape[0] * x_prompt.shape[1]
    hp, hs = x_prompt, x_sample
    vfirst_p = None
    vfirst_s = None
    p_shift, p_wkv, p_conv, p_delta, p_ret = [], [], [], [], []
    s_shift, s_wkv, s_conv, s_delta, s_ret = [], [], [], [], []
    for i in range(DEPTH):
        kind, j = i % N_MIXERS, i // N_MIXERS
        if kind == 0:
            prm = (rwkv_mu[j], rwkv_w_rkv[j], rwkv_w_o[j], rwkv_w0[j], rwkv_w1[j], rwkv_w2[j],
                   rwkv_a0[j], rwkv_a1[j], rwkv_a2[j], rwkv_g1[j], rwkv_g2[j], rwkv_k_k[j],
                   rwkv_k_a[j], rwkv_r_k[j], rwkv_lnx_g[j], rwkv_lnx_b[j])
            v_res = None if j == 0 else (rwkv_v0[j - 1], rwkv_v1[j - 1], rwkv_v2[j - 1])
            mp, sh_p, st_p, vfirst_p = rwkv7_time_mix(
                hp, jnp.zeros((n_p, D_MODEL), hp.dtype),
                jnp.zeros((n_p, A_HEADS, A_HEAD, A_HEAD), F32), vfirst_p, v_res, *prm)
            ms, sh_s, st_s, vfirst_s = rwkv7_time_mix(
                hs, state_rwkv_shift[j], state_rwkv_wkv[j], vfirst_s, v_res, *prm)
            p_shift.append(sh_p)
            p_wkv.append(st_p)
            s_shift.append(sh_s)
            s_wkv.append(st_s)
        elif kind == 1:
            prm = (gdn_w_in[j], gdn_conv_w[j], gdn_a_log[j], gdn_dt_bias[j], gdn_norm_w[j], gdn_w_o[j])
            mp, cv_p, st_p = gated_deltanet_mix(
                hp, jnp.zeros((n_p, B_CONV - 1, B_CONV_CH), hp.dtype),
                jnp.zeros((n_p, B_HEADS, B_DK, B_DV), F32), *prm)
            ms, cv_s, st_s = gated_deltanet_mix(hs, state_gdn_conv[j], state_gdn_delta[j], *prm)
            p_conv.append(cv_p)
            p_delta.append(st_p)
            s_conv.append(cv_s)
            s_delta.append(st_s)
        else:
            mp, st_p = retention_mix(hp, jnp.zeros((n_p, C_HEADS, C_DK, C_DV), F32), 0,
                                     ret_w_in[j], ret_w_o[j])
            ms, st_s = retention_mix(hs, state_ret[j], PAST_LEN, ret_w_in[j], ret_w_o[j])
            p_ret.append(st_p)
            s_ret.append(st_s)
        hp = layer_norm(DEEPNORM_ALPHA * hp + mp, ln_gain[i, 0], ln_bias[i, 0])
        hs = layer_norm(DEEPNORM_ALPHA * hs + ms, ln_gain[i, 0], ln_bias[i, 0])
        xt = jnp.concatenate([hp.reshape(-1, D_MODEL), hs.reshape(-1, D_MODEL)], axis=0)
        yt = routed_moe(xt, moe_w_router, moe_b_router, moe_w1, moe_b1, moe_w2, moe_b2, i)
        xt = layer_norm(DEEPNORM_ALPHA * xt + yt, ln_gain[i, 1], ln_bias[i, 1])
        hp = xt[:n_ptok].reshape(x_prompt.shape)
        hs = xt[n_ptok:].reshape(x_sample.shape)
    stk = lambda rows, like: jnp.stack(rows).astype(like.dtype)
    return (hp, hs,
            stk(p_shift, state_rwkv_shift), stk(p_wkv, state_rwkv_wkv),
            stk(p_conv, state_gdn_conv), stk(p_delta, state_gdn_delta), stk(p_ret, state_ret),
            stk(s_shift, state_rwkv_shift), stk(s_wkv, state_rwkv_wkv),
            stk(s_conv, state_gdn_conv), stk(s_delta, state_gdn_delta), stk(s_ret, state_ret))
```

```python
import functools
import math

import jax
import jax.numpy as jnp
from jax import lax
from jax.experimental import pallas as pl
from jax.experimental.pallas import tpu as pltpu

F32 = jnp.float32
BF16 = jnp.bfloat16

D_MODEL = 4096
DEPTH = 4
PAST_LEN = 16384
N_MIXERS = 3

DEEPNORM_ALPHA = (2.0 * DEPTH) ** 0.25
LN_EPS = 1e-5

A_HEAD = 64
A_HEADS = D_MODEL // A_HEAD
A_GN_EPS = 64e-5

B_DK = 128
B_HEADS = D_MODEL // B_DK
B_DV = 2 * B_DK
B_QK = B_HEADS * B_DK
B_V = B_HEADS * B_DV
B_CONV = 4
B_CONV_CH = 2 * B_QK + B_V
B_CHUNK = 64

C_DK = 256
C_HEADS = D_MODEL // C_DK
C_DV = 2 * C_DK
C_QK = C_HEADS * C_DK
C_V = C_HEADS * C_DV
C_CHUNK = 128
ROT_BASE = 10000.0

N_EXPERTS = 32
TOP_K = 4
D_EXPERT = D_MODEL // 4
SWIGLU_ALPHA = 1.702
SWIGLU_LIMIT = 7.0

LANES = 128

MOE_ROWS = 256
MOE_TN1 = 1024
MOE_TN2 = 2048
LANE_PAIR_GROUP = 256
VMEM_LIMIT = 56 * 1024 * 1024

WKV_CHUNK = 64
WKV_PAIRS = 4
GDN_HEADS = 8
RET_HEADS = 4


def layer_norm(x, g, b):
    xf = x.astype(F32)
    xc = xf - xf.mean(-1, keepdims=True)
    var = jnp.mean(xc * xc, -1, keepdims=True)
    y = xc * lax.rsqrt(var + LN_EPS) * g.astype(F32) + b.astype(F32)
    return y.astype(x.dtype)


def head_norm(x, eps):
    xc = x - x.mean(-1, keepdims=True)
    return xc * lax.rsqrt(jnp.mean(xc * xc, -1, keepdims=True) + eps)


def rms_norm(x, eps):
    return x * lax.rsqrt(jnp.mean(x * x, -1, keepdims=True) + eps)


def l2_normalize(x, eps):
    return x * lax.rsqrt(jnp.sum(x * x, -1, keepdims=True) + eps)


def _dot(a, b):
    return jnp.dot(a.astype(BF16), b.astype(BF16), preferred_element_type=F32)


def _dot_nt(a, b):
    return lax.dot_general(a.astype(BF16), b.astype(BF16), (((1,), (1,)), ((), ())),
                           preferred_element_type=F32)


def _dot_tn(a, b):
    return lax.dot_general(a.astype(BF16), b.astype(BF16), (((0,), (0,)), ((), ())),
                           preferred_element_type=F32)


def _split3(x):
    h1 = x.astype(BF16)
    r1 = x - h1.astype(F32)
    h2 = r1.astype(BF16)
    h3 = (r1 - h2.astype(F32)).astype(BF16)
    return h1, h2, h3


def _cumsum_rows(tril_bf, x):
    h1, h2, h3 = _split3(x)
    d = lambda y: jnp.dot(tril_bf, y, preferred_element_type=F32)
    return d(h1) + (d(h2) + d(h3))


def _level_masks(c):
    row = lax.broadcasted_iota(jnp.int32, (c, c), 0)
    col = lax.broadcasted_iota(jnp.int32, (c, c), 1)
    masks = []
    size = 1
    while size < c:
        same_big = (row // (2 * size)) == (col // (2 * size))
        same_small = (row // size) == (col // size)
        masks.append(same_big & jnp.logical_not(same_small) & (row > col))
        size *= 2
    return masks


def _unit_lower_inverses(a_strict, eye, masks):
    n = range(len(a_strict))
    t = [eye + jnp.where(masks[0], a_strict[i], 0.0) for i in n]
    for m in masks[1:]:
        tn = [_dot(t[i], jnp.where(m, a_strict[i], 0.0)) for i in n]
        t = [t[i] + _dot(tn[i], t[i]) for i in n]
    return t


def _wkv_kernel(r_ref, lw_ref, k_ref, v_ref, kk_ref, a_ref, y_ref, s_out_ref, s_scr):
    c_idx = pl.program_id(2)
    n_c = pl.num_programs(2)
    c = r_ref.shape[0]
    n_pairs = r_ref.shape[1] // LANES
    pairs = range(n_pairs)
    heads = [(p, h) for p in pairs for h in range(2)]

    @pl.when(c_idx == 0)
    def _():
        s_scr[...] = jnp.zeros_like(s_scr)

    row = lax.broadcasted_iota(jnp.int32, (c, c), 0)
    col = lax.broadcasted_iota(jnp.int32, (c, c), 1)
    incl = row >= col
    strict = row > col
    tril_bf = incl.astype(BF16)
    eye = (row == col).astype(F32)
    lane = lax.broadcasted_iota(jnp.int32, (1, LANES), 1)
    m0 = lane < A_HEAD
    bd = (lax.broadcasted_iota(jnp.int32, (LANES, LANES), 0) // A_HEAD
          == lax.broadcasted_iota(jnp.int32, (LANES, LANES), 1) // A_HEAD)
    masks = _level_masks(c)

    sl = [slice(p * LANES, (p + 1) * LANES) for p in pairs]
    r = [r_ref[:, sl[p]] for p in pairs]
    lw = [lw_ref[:, sl[p]] for p in pairs]
    k = [k_ref[:, sl[p]] for p in pairs]
    v = [v_ref[:, sl[p]] for p in pairs]
    kk = [kk_ref[:, sl[p]] for p in pairs]
    a = [a_ref[:, sl[p]] for p in pairs]
    s0 = [s_scr[p] for p in pairs]

    cum = [_cumsum_rows(tril_bf, lw[p]) for p in pairs]
    pw = [jnp.exp(cum[p]) for p in pairs]
    inv_pw = [jnp.exp(-cum[p]) for p in pairs]
    pw_last = [pw[p][c - 1:c, :] for p in pairs]
    at = [-kk[p] * jnp.exp(cum[p] - lw[p]) for p in pairs]
    bt = [kk[p] * a[p] * inv_pw[p] for p in pairs]
    kt = [k[p] * inv_pw[p] for p in pairs]
    rt = [r[p] * pw[p] for p in pairs]

    def gram_lhs(p):
        zero = jnp.zeros_like(at[p])
        return jnp.concatenate([jnp.where(m0, at[p], zero), jnp.where(m0, zero, at[p]),
                                jnp.where(m0, rt[p], zero), jnp.where(m0, zero, rt[p])], axis=0)
    lhs = [gram_lhs(p) for p in pairs]
    g_b = [_dot_nt(lhs[p], bt[p]) for p in pairs]
    g_k = [_dot_nt(lhs[p], kt[p]) for p in pairs]
    a_ab = [jnp.where(strict, g_b[p][h * c:(h + 1) * c], 0.0) for p, h in heads]
    a_ak = [jnp.where(strict, g_k[p][h * c:(h + 1) * c], 0.0) for p, h in heads]
    a_rb = [jnp.where(incl, g_b[p][(2 + h) * c:(3 + h) * c], 0.0) for p, h in heads]
    a_rk = [jnp.where(incl, g_k[p][(2 + h) * c:(3 + h) * c], 0.0) for p, h in heads]

    akv = [_dot(a_ak[i], v[p]) for i, (p, h) in enumerate(heads)]
    y0 = [_dot(a_rk[i], v[p]) for i, (p, h) in enumerate(heads)]
    t = _unit_lower_inverses(a_ab, eye, masks)
    wu = [_dot(t[i], jnp.concatenate([at[p], akv[i]], axis=1)) for i, (p, h) in enumerate(heads)]
    w = [jnp.where(m0, wu[2 * p][:, :LANES], wu[2 * p + 1][:, :LANES]) for p in pairs]
    u0 = [jnp.where(m0, wu[2 * p][:, LANES:], wu[2 * p + 1][:, LANES:]) for p in pairs]

    wr = [_dot_nt(jnp.concatenate([w[p], rt[p]], axis=0), s0[p]) for p in pairs]
    u = [wr[p][:c] + u0[p] for p in pairs]
    aru = [_dot(a_rb[i], u[p]) for i, (p, h) in enumerate(heads)]
    upd = [_dot_tn(jnp.concatenate([u[p], v[p]], axis=0),
                   jnp.concatenate([bt[p] * pw_last[p], kt[p] * pw_last[p]], axis=0)) for p in pairs]
    for p in pairs:
        y_ref[:, sl[p]] = (wr[p][c:] + jnp.where(m0, y0[2 * p], y0[2 * p + 1])
                           + jnp.where(m0, aru[2 * p], aru[2 * p + 1]))
        s_scr[p] = s0[p] * pw_last[p] + jnp.where(bd, upd[p], 0.0)

    @pl.when(c_idx == n_c - 1)
    def _():
        for p in pairs:
            s_fin = s_scr[p]
            s_out_ref[2 * p] = s_fin[:A_HEAD, :A_HEAD]
            s_out_ref[2 * p + 1] = s_fin[A_HEAD:, A_HEAD:]


def wkv7_chunked(r, lw, k, v, kk, a):
    nb, t_len, d = r.shape
    lanes = LANES * WKV_PAIRS
    grid = (nb, d // lanes, t_len // WKV_CHUNK)
    blk = pl.BlockSpec((None, WKV_CHUNK, lanes), lambda b, g, c: (b, c, g))
    return pl.pallas_call(
        _wkv_kernel,
        grid=grid,
        in_specs=[blk] * 6,
        out_specs=[blk, pl.BlockSpec((None, 2 * WKV_PAIRS, A_HEAD, A_HEAD), lambda b, g, c: (b, g, 0, 0))],
        out_shape=[jax.ShapeDtypeStruct((nb, t_len, d), F32),
                   jax.ShapeDtypeStruct((nb, d // A_HEAD, A_HEAD, A_HEAD), F32)],
        scratch_shapes=[pltpu.VMEM((WKV_PAIRS, LANES, LANES), F32)],
        compiler_params=pltpu.CompilerParams(dimension_semantics=("parallel", "parallel", "arbitrary")),
        name="wkv7_chunked",
    )(r, lw, k, v, kk, a)


def wkv7_scan(r, w, k, v, kk, a, s0):
    def step(s, inp):
        r_t, w_t, k_t, v_t, kk_t, a_t = inp
        sa = jnp.einsum('bhij,bhj->bhi', s, -kk_t)
        s = (s * w_t[:, :, None, :] + sa[..., None] * (kk_t * a_t)[:, :, None, :]
             + v_t[..., None] * k_t[:, :, None, :])
        return s, jnp.einsum('bhij,bhj->bhi', s, r_t)
    xs = tuple(jnp.moveaxis(t, 1, 0) for t in (r, w, k, v, kk, a))
    s, y = lax.scan(step, s0, xs)
    return jnp.moveaxis(y, 0, 1), s


def rwkv7_time_mix(x, shift0, wkv0, v_first, v_res, mu, w_rkv, w_o, w0, w1, w2,
                   a0, a1, a2, g1, g2, k_k, k_a, r_k, lnx_g, lnx_b):
    nb, t_len, d = x.shape
    x_prev = jnp.concatenate([shift0[:, None, :].astype(x.dtype), x[:, :-1]], axis=1)
    xx = x_prev - x
    mix = lambda j: x + xx * mu[j]
    r = mix(0) @ w_rkv[0]
    k = mix(1) @ w_rkv[1]
    xv = mix(2)
    v = xv @ w_rkv[2]
    w_log = -jax.nn.softplus(-(w0 + jnp.tanh(mix(3) @ w1) @ w2).astype(F32)) - 0.5
    a = jax.nn.sigmoid((a0 + (mix(4) @ a1) @ a2).astype(F32))
    g = jax.nn.sigmoid(mix(5) @ g1) @ g2
    if v_first is None:
        v_first = v
    else:
        v0, v1, v2 = v_res
        v = v + (v_first - v) * jax.nn.sigmoid(v0 + (xv @ v1) @ v2)
    hs = lambda t: t.astype(F32).reshape(nb, t_len, A_HEADS, A_HEAD)
    flat = lambda t: t.reshape(nb, t_len, d)
    r, k, v, a = hs(r), hs(k), hs(v), hs(a)
    kk = l2_normalize(k * k_k.astype(F32).reshape(A_HEADS, A_HEAD), 1e-12)
    k = k * (1.0 + (a - 1.0) * k_a.astype(F32).reshape(A_HEADS, A_HEAD))
    if wkv0 is None:
        y, wkv = wkv7_chunked(flat(r), -jnp.exp(w_log), flat(k), flat(v), flat(kk), flat(a))
        y = hs(y)
    else:
        decay = hs(jnp.exp(-jnp.exp(w_log)))
        y, wkv = wkv7_scan(r, decay, k, v, kk, a, wkv0.astype(F32))
    y = head_norm(y, A_GN_EPS).reshape(nb, t_len, d) * lnx_g.astype(F32) + lnx_b.astype(F32)
    bonus = jnp.sum(r * k * r_k.astype(F32), -1, keepdims=True) * v
    y = (y + bonus.reshape(nb, t_len, d)) * g.astype(F32)
    return y.astype(x.dtype) @ w_o, x[:, -1], wkv, v_first


def _gdn_kernel(q_ref, k_ref, v_ref, z_ref, beta_ref, g_ref, nw_ref, o_ref, s_out_ref, s_scr):
    c_idx = pl.program_id(2)
    n_c = pl.num_programs(2)
    c = q_ref.shape[0]
    nh = q_ref.shape[1] // B_DK
    hs = range(nh)

    @pl.when(c_idx == 0)
    def _():
        s_scr[...] = jnp.zeros_like(s_scr)

    row = lax.broadcasted_iota(jnp.int32, (c, c), 0)
    col = lax.broadcasted_iota(jnp.int32, (c, c), 1)
    incl = row >= col
    strict = row > col
    tril_bf = incl.astype(BF16)
    triu_bf = (row <= col).astype(BF16)
    eye = (row == col).astype(F32)
    masks = _level_masks(c)

    g = g_ref[...]
    beta = beta_ref[...]
    gc = _cumsum_rows(tril_bf, g)
    g3 = _split3(g)
    tn = lambda x: lax.dot_general(x, triu_bf, (((0,), (0,)), ((), ())), preferred_element_type=F32)
    gc_t = tn(g3[0]) + (tn(g3[1]) + tn(g3[2]))

    q = [q_ref[:, h * B_DK:(h + 1) * B_DK] for h in hs]
    k = [k_ref[:, h * B_DK:(h + 1) * B_DK] for h in hs]
    v = [v_ref[:, h * B_DV:(h + 1) * B_DV] for h in hs]
    z = [z_ref[:, h * B_DV:(h + 1) * B_DV] for h in hs]
    s0 = [s_scr[h] for h in hs]
    nw = nw_ref[...]

    gcol = [gc[:, h:h + 1] for h in hs]
    bcol = [beta[:, h:h + 1] for h in hs]
    glast = [gc[c - 1:c, h:h + 1] for h in hs]
    decay = [jnp.where(incl, jnp.exp(jnp.where(incl, gcol[h] - gc_t[h:h + 1, :], 0.0)), 0.0) for h in hs]
    egc = [jnp.exp(gcol[h]) for h in hs]
    kb = [k[h] * bcol[h] for h in hs]
    gram = [_dot_nt(jnp.concatenate([kb[h], q[h]], axis=0), k[h]) for h in hs]
    a_neg = [jnp.where(strict, -gram[h][:c] * decay[h], 0.0) for h in hs]
    qk = [gram[h][c:] * decay[h] for h in hs]
    t = _unit_lower_inverses(a_neg, eye, masks)
    sol = [_dot(t[h], jnp.concatenate([v[h] * bcol[h], kb[h] * egc[h]], axis=1)) for h in hs]
    wq = [_dot(jnp.concatenate([sol[h][:, B_DV:], q[h] * egc[h]], axis=0), s0[h]) for h in hs]
    v_new = [sol[h][:, :B_DV] - wq[h][:c] for h in hs]
    o = [wq[h][c:] + _dot(qk[h], v_new[h]) for h in hs]
    upd = [_dot_tn(k[h] * jnp.exp(glast[h] - gcol[h]), v_new[h]) for h in hs]
    for h in hs:
        on = o[h] * lax.rsqrt(jnp.mean(o[h] * o[h], axis=-1, keepdims=True) + 1e-6) * nw
        o_ref[:, h * B_DV:(h + 1) * B_DV] = on * (z[h] * jax.nn.sigmoid(z[h]))
        s_scr[h] = s0[h] * jnp.exp(glast[h]) + upd[h]

    @pl.when(c_idx == n_c - 1)
    def _():
        s_out_ref[...] = s_scr[...]


def gdn_chunked(q, k, v, proj, beta_g, g_g, norm_w):
    nb, t_len, _ = q.shape
    n_groups = B_HEADS // GDN_HEADS
    grid = (nb, n_groups, t_len // B_CHUNK)
    v_width = GDN_HEADS * B_DV
    z_first = B_CONV_CH // v_width
    qk_blk = pl.BlockSpec((None, B_CHUNK, GDN_HEADS * B_DK), lambda b, g, c: (b, c, g))
    v_blk = pl.BlockSpec((None, B_CHUNK, v_width), lambda b, g, c: (b, c, g))
    z_blk = pl.BlockSpec((None, B_CHUNK, v_width), lambda b, g, c: (b, c, z_first + g))
    sc_blk = pl.BlockSpec((None, None, B_CHUNK, GDN_HEADS), lambda b, g, c: (b, g, c, 0))
    return pl.pallas_call(
        _gdn_kernel,
        grid=grid,
        in_specs=[qk_blk, qk_blk, v_blk, z_blk, sc_blk, sc_blk,
                  pl.BlockSpec((1, B_DV), lambda b, g, c: (0, 0))],
        out_specs=[v_blk, pl.BlockSpec((None, GDN_HEADS, B_DK, B_DV), lambda b, g, c: (b, g, 0, 0))],
        out_shape=[jax.ShapeDtypeStruct((nb, t_len, B_V), F32),
                   jax.ShapeDtypeStruct((nb, B_HEADS, B_DK, B_DV), F32)],
        scratch_shapes=[pltpu.VMEM((GDN_HEADS, B_DK, B_DV), F32)],
        compiler_params=pltpu.CompilerParams(dimension_semantics=("parallel", "parallel", "arbitrary")),
        name="gdn_chunked",
    )(q, k, v, proj, beta_g, g_g, norm_w.reshape(1, B_DV))


def gated_delta_chunked(q, k, v, beta, g, s0):
    nb, nh, t_len, dk = q.shape
    dv = v.shape[-1]
    c = math.gcd(t_len, B_CHUNK)
    n = t_len // c
    q, k, v = (t.reshape(nb, nh, n, c, t.shape[-1]) for t in (q, k, v))
    beta, g = (t.reshape(nb, nh, n, c) for t in (beta, g))
    gc = jnp.cumsum(g, axis=-1)
    causal = jnp.tril(jnp.ones((c, c), dtype=bool))
    eye = jnp.eye(c, dtype=jnp.float32)
    diff = gc[..., :, None] - gc[..., None, :]
    decay = jnp.where(causal, jnp.exp(jnp.where(causal, diff, 0.0)), 0.0)
    kb = k * beta[..., None]
    lower = jnp.einsum('bhncd,bhnsd->bhncs', kb, k) * decay * (1.0 - eye)
    rhs = jnp.concatenate([v * beta[..., None], kb * jnp.exp(gc)[..., None]], axis=-1)
    sol = lax.linalg.triangular_solve(lower + eye, rhs, left_side=True, lower=True)
    u, w = sol[..., :dv], sol[..., dv:]
    qk = jnp.einsum('bhncd,bhnsd->bhncs', q, k) * decay
    q_dec = q * jnp.exp(gc)[..., None]
    k_dec = k * jnp.exp(gc[..., -1:] - gc)[..., None]
    g_tot = jnp.exp(gc[..., -1])[..., None, None]

    def step(s, inp):
        u_c, w_c, qk_c, qd_c, kd_c, gt_c = inp
        v_new = u_c - jnp.einsum('bhcd,bhde->bhce', w_c, s)
        o = jnp.einsum('bhcd,bhde->bhce', qd_c, s) + jnp.einsum('bhcs,bhse->bhce', qk_c, v_new)
        s = s * gt_c + jnp.einsum('bhcd,bhce->bhde', kd_c, v_new)
        return s, o
    xs = tuple(jnp.moveaxis(t, 2, 0) for t in (u, w, qk, q_dec, k_dec, g_tot))
    s, o = lax.scan(step, s0, xs)
    return jnp.moveaxis(o, 0, 2).reshape(nb, nh, t_len, dv), s


def gated_deltanet_mix(x, conv0, s0, w_in, conv_w, a_log, dt_bias, norm_w, w_o):
    nb, t_len, _ = x.shape
    proj = x @ w_in
    qkv = proj[..., :B_CONV_CH]
    a_in = proj[..., B_CONV_CH + B_V:B_CONV_CH + B_V + B_HEADS]
    b_in = proj[..., B_CONV_CH + B_V + B_HEADS:]
    if conv0 is None:
        conv0 = jnp.zeros((nb, B_CONV - 1, B_CONV_CH), qkv.dtype)
    xp = jnp.concatenate([conv0.astype(qkv.dtype), qkv], axis=1)
    conv = xp[:, 0:t_len] * conv_w[0]
    for j in range(1, B_CONV):
        conv = conv + xp[:, j:j + t_len] * conv_w[j]
    conv = jax.nn.silu(conv.astype(F32))
    new_conv = xp[:, t_len:]
    q = conv[..., :B_QK].reshape(nb, t_len, B_HEADS, B_DK)
    k = conv[..., B_QK:2 * B_QK].reshape(nb, t_len, B_HEADS, B_DK)
    v = conv[..., 2 * B_QK:].reshape(nb, t_len, B_HEADS, B_DV)
    q = l2_normalize(q, 1e-6) * (B_DK ** -0.5)
    k = l2_normalize(k, 1e-6)
    beta = jax.nn.sigmoid(b_in.astype(F32))
    g = -jnp.exp(a_log.astype(F32)) * jax.nn.softplus(a_in.astype(F32) + dt_bias.astype(F32))
    if s0 is None:
        grp = lambda t: jnp.transpose(t.reshape(nb, t_len, B_HEADS // GDN_HEADS, GDN_HEADS), (0, 2, 1, 3))
        o, s = gdn_chunked(q.reshape(nb, t_len, B_QK), k.reshape(nb, t_len, B_QK),
                           v.reshape(nb, t_len, B_V), proj, grp(beta), grp(g), norm_w.astype(F32))
        return o.astype(x.dtype) @ w_o, new_conv, s
    z = proj[..., B_CONV_CH:B_CONV_CH + B_V]
    tr = lambda t: jnp.swapaxes(t, 1, 2)
    o, s = gated_delta_chunked(tr(q), tr(k), tr(v), tr(beta), tr(g), s0.astype(F32))
    o = rms_norm(tr(o), 1e-6) * norm_w.astype(F32)
    o = o * jax.nn.silu(z.astype(F32).reshape(nb, t_len, B_HEADS, B_DV))
    return o.reshape(nb, t_len, B_V).astype(x.dtype) @ w_o, new_conv, s


def _ret_kernel(lg_ref, q_ref, k_ref, v_ref, gate_ref, cos_ref, sin_ref, o_ref, s_out_ref, s_scr):
    g_idx = pl.program_id(1)
    c_idx = pl.program_id(2)
    n_c = pl.num_programs(2)
    c = q_ref.shape[0]
    nh = q_ref.shape[1] // C_DK
    hs = range(nh)

    @pl.when(c_idx == 0)
    def _():
        s_scr[...] = jnp.zeros_like(s_scr)

    width = q_ref.shape[1]
    lane = lax.broadcasted_iota(jnp.int32, (1, width), 1)
    even = (lane % 2) == 0
    cos = jnp.concatenate([cos_ref[...]] * nh, axis=1)
    sin = jnp.concatenate([sin_ref[...]] * nh, axis=1)

    def rotate(x):
        partner = jnp.where(even, pltpu.roll(x, width - 1, axis=1), pltpu.roll(x, 1, axis=1))
        return x * cos + partner * sin

    qr = rotate(q_ref[...])
    kr = rotate(k_ref[...]) * (C_DK ** -0.5)
    row = lax.broadcasted_iota(jnp.int32, (c, c), 0)
    col = lax.broadcasted_iota(jnp.int32, (c, c), 1)
    rel = (row - col).astype(F32)
    causal = row >= col
    idx = lax.broadcasted_iota(jnp.int32, (c, 1), 0).astype(F32)

    lg = [lg_ref[g_idx * nh + h] for h in hs]
    q = [qr[:, h * C_DK:(h + 1) * C_DK] for h in hs]
    k = [kr[:, h * C_DK:(h + 1) * C_DK] for h in hs]
    v = [v_ref[:, h * C_DV:(h + 1) * C_DV] for h in hs]
    s0 = [s_scr[h] for h in hs]
    dmat = [jnp.where(causal, jnp.exp(jnp.where(causal, rel, 0.0) * lg[h]), 0.0) for h in hs]
    inner = [_dot_nt(q[h], k[h]) * dmat[h] for h in hs]
    o = [_dot(inner[h], v[h]) + _dot(q[h] * jnp.exp((idx + 1.0) * lg[h]), s0[h]) for h in hs]
    upd = [_dot_tn(k[h] * jnp.exp((c - 1.0 - idx) * lg[h]), v[h]) for h in hs]
    for h in hs:
        oc = o[h] - jnp.mean(o[h], axis=-1, keepdims=True)
        on = oc * lax.rsqrt(jnp.mean(oc * oc, axis=-1, keepdims=True) + 1e-5)
        gt = gate_ref[:, h * C_DV:(h + 1) * C_DV]
        o_ref[:, h * C_DV:(h + 1) * C_DV] = gt * jax.nn.sigmoid(gt) * on
        s_scr[h] = s0[h] * jnp.exp(c * lg[h]) + upd[h]

    @pl.when(c_idx == n_c - 1)
    def _():
        s_out_ref[...] = s_scr[...]


def retention_chunked_pallas(proj, cos, sin_signed, log_gamma):
    nb, t_len, _ = proj.shape
    n_groups = C_HEADS // RET_HEADS
    grid = (nb, n_groups, t_len // C_CHUNK)
    qk_width = RET_HEADS * C_DK
    v_width = RET_HEADS * C_DV
    k_first = C_QK // qk_width
    v_first = 2 * C_QK // v_width
    gate_first = (2 * C_QK + C_V) // v_width
    q_blk = pl.BlockSpec((None, C_CHUNK, qk_width), lambda b, g, c, lg: (b, c, g))
    k_blk = pl.BlockSpec((None, C_CHUNK, qk_width), lambda b, g, c, lg: (b, c, k_first + g))
    v_blk = pl.BlockSpec((None, C_CHUNK, v_width), lambda b, g, c, lg: (b, c, v_first + g))
    gate_blk = pl.BlockSpec((None, C_CHUNK, v_width), lambda b, g, c, lg: (b, c, gate_first + g))
    o_blk = pl.BlockSpec((None, C_CHUNK, v_width), lambda b, g, c, lg: (b, c, g))
    tab_blk = pl.BlockSpec((C_CHUNK, C_DK), lambda b, g, c, lg: (c, 0))
    return pl.pallas_call(
        _ret_kernel,
        grid_spec=pltpu.PrefetchScalarGridSpec(
            num_scalar_prefetch=1,
            grid=grid,
            in_specs=[q_blk, k_blk, v_blk, gate_blk, tab_blk, tab_blk],
            out_specs=[o_blk, pl.BlockSpec((None, RET_HEADS, C_DK, C_DV), lambda b, g, c, lg: (b, g, 0, 0))],
            scratch_shapes=[pltpu.VMEM((RET_HEADS, C_DK, C_DV), F32)],
        ),
        out_shape=[jax.ShapeDtypeStruct((nb, t_len, C_V), F32),
                   jax.ShapeDtypeStruct((nb, C_HEADS, C_DK, C_DV), F32)],
        compiler_params=pltpu.CompilerParams(dimension_semantics=("parallel", "parallel", "arbitrary"),
                                             vmem_limit_bytes=VMEM_LIMIT),
        name="retention_chunked",
    )(log_gamma, proj, proj, proj, proj, cos, sin_signed)


def rotary_tables(t_len, start):
    half = C_DK // 2
    inv_freq = 1.0 / (ROT_BASE ** jnp.linspace(0.0, 1.0, half, dtype=jnp.float32))
    pos = start + jnp.arange(t_len, dtype=F32)
    ang = pos[:, None] * inv_freq[None, :]
    sin, cos = jnp.sin(ang), jnp.cos(ang)
    return jnp.repeat(cos, 2, axis=1), jnp.stack([-sin, sin], axis=-1).reshape(t_len, C_DK)


def rel_rotate(x, pos):
    half = x.shape[-1] // 2
    inv_freq = 1.0 / (ROT_BASE ** jnp.linspace(0.0, 1.0, half, dtype=jnp.float32))
    ang = pos[:, None] * inv_freq[None, :]
    sin, cos = jnp.sin(ang), jnp.cos(ang)
    x1, x2 = x[..., 0::2], x[..., 1::2]
    return jnp.stack([x1 * cos - x2 * sin, x2 * cos + x1 * sin], axis=-1).reshape(x.shape)


def retention_chunked(q, k, v, log_gamma, s0):
    nb, nh, t_len, _ = q.shape
    dv = v.shape[-1]
    c = math.gcd(t_len, C_CHUNK)
    n = t_len // c
    q, k, v = (t.reshape(nb, nh, n, c, t.shape[-1]) for t in (q, k, v))
    idx = jnp.arange(c, dtype=jnp.float32)
    lg = log_gamma[:, None]
    rel = idx[:, None] - idx[None, :]
    causal = rel >= 0
    dmat = jnp.where(causal, jnp.exp(jnp.where(causal, rel, 0.0)[None] * lg[..., None]), 0.0)
    inner = jnp.einsum('bhncd,bhnsd->bhncs', q, k) * dmat[:, None]
    o_inner = jnp.einsum('bhncs,bhnse->bhnce', inner, v)
    q_dec = q * jnp.exp((idx + 1.0) * lg)[:, None, :, None]
    k_dec = k * jnp.exp((c - 1.0 - idx) * lg)[:, None, :, None]
    chunk_decay = jnp.exp(c * log_gamma)[:, None, None]

    def step(s, inp):
        qd, kd, vc, oi = inp
        o = oi + jnp.einsum('bhcd,bhde->bhce', qd, s)
        s = s * chunk_decay + jnp.einsum('bhcd,bhce->bhde', kd, vc)
        return s, o
    xs = tuple(jnp.moveaxis(t, 2, 0) for t in (q_dec, k_dec, v, o_inner))
    s, o = lax.scan(step, s0, xs)
    return jnp.moveaxis(o, 0, 2).reshape(nb, nh, t_len, dv), s


def retention_mix(x, s0, start, w_in, w_o):
    nb, t_len, _ = x.shape
    proj = x @ w_in
    log_gamma = jnp.log(1.0 - 2.0 ** (-5.0 - jnp.arange(C_HEADS, dtype=F32)))
    if s0 is None:
        cos, sin_signed = rotary_tables(t_len, start)
        o, s = retention_chunked_pallas(proj.astype(F32), cos, sin_signed, log_gamma)
        return o.astype(x.dtype) @ w_o, s
    heads = lambda t, hd: jnp.swapaxes(t.astype(F32).reshape(nb, t_len, C_HEADS, hd), 1, 2)
    q = heads(proj[..., :C_QK], C_DK)
    k = heads(proj[..., C_QK:2 * C_QK], C_DK)
    v = heads(proj[..., 2 * C_QK:2 * C_QK + C_V], C_DV)
    gate = proj[..., 2 * C_QK + C_V:]
    pos = start + jnp.arange(t_len, dtype=F32)
    q = rel_rotate(q, pos)
    k = rel_rotate(k, pos) * (C_DK ** -0.5)
    o, s = retention_chunked(q, k, v, log_gamma, s0.astype(F32))
    o = head_norm(jnp.swapaxes(o, 1, 2), 1e-5).reshape(nb, t_len, C_V)
    return (jax.nn.silu(gate.astype(F32)) * o).astype(x.dtype) @ w_o, s


def _moe_up_kernel(blk_e_ref, blk_first_ref, n_used_ref, x_ref, w1_ref, b1_ref, sel_ref,
                   o_ref, wbf_ref):
    del blk_e_ref
    i = pl.program_id(1)
    tn = w1_ref.shape[-1]

    @pl.when(i >= n_used_ref[0])
    def _():
        o_ref[...] = jnp.zeros_like(o_ref)

    @pl.when(i < n_used_ref[0])
    def _():
        @pl.when(blk_first_ref[i] == 1)
        def _():
            wbf_ref[...] = w1_ref[...].astype(BF16)

        h = jnp.dot(x_ref[...], wbf_ref[...], preferred_element_type=F32) + b1_ref[...]
        glu = jnp.minimum(h, SWIGLU_LIMIT)
        lin = jnp.clip(pltpu.roll(h, tn - 1, axis=1), -SWIGLU_LIMIT, SWIGLU_LIMIT)
        act = (glu * jax.nn.sigmoid(SWIGLU_ALPHA * glu) * (lin + 1.0)).astype(BF16)
        half = LANE_PAIR_GROUP // 2
        for c in range(tn // LANE_PAIR_GROUP):
            grp = act[:, c * LANE_PAIR_GROUP:(c + 1) * LANE_PAIR_GROUP]
            o_ref[:, c * half:(c + 1) * half] = jnp.dot(
                grp, sel_ref[...], preferred_element_type=F32).astype(o_ref.dtype)


def _moe_down_kernel(blk_e_ref, blk_first_ref, n_used_ref, h_ref, w2_ref, b2_ref, o_ref, wbf_ref):
    del blk_e_ref
    i = pl.program_id(1)

    @pl.when(i >= n_used_ref[0])
    def _():
        o_ref[...] = jnp.zeros_like(o_ref)

    @pl.when(i < n_used_ref[0])
    def _():
        @pl.when(blk_first_ref[i] == 1)
        def _():
            wbf_ref[...] = w2_ref[...].astype(BF16)

        o_ref[...] = jnp.dot(h_ref[...], wbf_ref[...], preferred_element_type=F32) + b2_ref[...]


def _moe_experts(x_rows, blk_e, blk_first, n_used, w1, b1, w2, b2, layer):
    n_rows = x_rows.shape[0]
    n_blocks = n_rows // MOE_ROWS
    d_up = 2 * D_EXPERT
    sel = (jnp.arange(LANE_PAIR_GROUP)[:, None] == 2 * jnp.arange(LANE_PAIR_GROUP // 2)[None, :]).astype(BF16)
    b1r = b1.reshape(DEPTH, N_EXPERTS, 1, d_up)
    b2r = b2.reshape(DEPTH, N_EXPERTS, 1, D_MODEL)
    row_blk = lambda j, i, be, bf, nu: (jnp.minimum(i, nu[0] - 1), 0)
    params = pltpu.CompilerParams(dimension_semantics=("arbitrary", "arbitrary"),
                                  vmem_limit_bytes=VMEM_LIMIT)

    act = pl.pallas_call(
        _moe_up_kernel,
        grid_spec=pltpu.PrefetchScalarGridSpec(
            num_scalar_prefetch=3,
            grid=(d_up // MOE_TN1, n_blocks),
            in_specs=[
                pl.BlockSpec((MOE_ROWS, D_MODEL), row_blk),
                pl.BlockSpec((None, None, D_MODEL, MOE_TN1),
                             lambda j, i, be, bf, nu: (layer, be[i], 0, j)),
                pl.BlockSpec((None, None, 1, MOE_TN1),
                             lambda j, i, be, bf, nu: (layer, be[i], 0, j)),
                pl.BlockSpec((LANE_PAIR_GROUP, LANE_PAIR_GROUP // 2),
                             lambda j, i, be, bf, nu: (0, 0)),
            ],
            out_specs=pl.BlockSpec((MOE_ROWS, MOE_TN1 // 2), lambda j, i, be, bf, nu: (i, j)),
            scratch_shapes=[pltpu.VMEM((D_MODEL, MOE_TN1), BF16)],
        ),
        out_shape=jax.ShapeDtypeStruct((n_rows, D_EXPERT), BF16),
        compiler_params=params,
        name="moe_up",
    )(blk_e, blk_first, n_used, x_rows, w1, b1r, sel)

    return pl.pallas_call(
        _moe_down_kernel,
        grid_spec=pltpu.PrefetchScalarGridSpec(
            num_scalar_prefetch=3,
            grid=(D_MODEL // MOE_TN2, n_blocks),
            in_specs=[
                pl.BlockSpec((MOE_ROWS, D_EXPERT), row_blk),
                pl.BlockSpec((None, None, D_EXPERT, MOE_TN2),
                             lambda j, i, be, bf, nu: (layer, be[i], 0, j)),
                pl.BlockSpec((None, None, 1, MOE_TN2),
                             lambda j, i, be, bf, nu: (layer, be[i], 0, j)),
            ],
            out_specs=pl.BlockSpec((MOE_ROWS, MOE_TN2), lambda j, i, be, bf, nu: (i, j)),
            scratch_shapes=[pltpu.VMEM((D_EXPERT, MOE_TN2), BF16)],
        ),
        out_shape=jax.ShapeDtypeStruct((n_rows, D_MODEL), F32),
        compiler_params=params,
        name="moe_down",
    )(blk_e, blk_first, n_used, act, w2, b2r)


def routed_moe(xt, w_router, b_router, w1, b1, w2, b2, layer):
    n_tok = xt.shape[0]
    logits = (xt @ w_router[layer] + b_router[layer]).astype(F32)
    top_val, top_idx = lax.top_k(logits, TOP_K)
    gates = jax.nn.softmax(top_val, axis=-1)
    tk = n_tok * TOP_K
    flat_e = top_idx.reshape(tk)
    order = jnp.argsort(flat_e)
    sorted_e = flat_e[order]
    counts = jnp.bincount(flat_e, length=N_EXPERTS)
    padded = (counts + MOE_ROWS - 1) // MOE_ROWS * MOE_ROWS
    start = jnp.cumsum(counts) - counts
    pad_end = jnp.cumsum(padded)
    pad_start = pad_end - padded
    dest_sorted = pad_start[sorted_e] + jnp.arange(tk, dtype=jnp.int32) - start[sorted_e]
    dest = jnp.zeros((tk,), jnp.int32).at[order].set(dest_sorted.astype(jnp.int32))
    n_blocks = -(-(tk + N_EXPERTS * (MOE_ROWS - 1)) // MOE_ROWS)
    n_rows = n_blocks * MOE_ROWS
    row_tok = jnp.full((n_rows,), n_tok, jnp.int32).at[dest].set(
        jnp.arange(tk, dtype=jnp.int32) // TOP_K)
    blk_e = jnp.minimum(
        jnp.searchsorted(pad_end, jnp.arange(n_blocks, dtype=jnp.int32) * MOE_ROWS, side='right'),
        N_EXPERTS - 1).astype(jnp.int32)
    blk_first = jnp.concatenate(
        [jnp.ones((1,), jnp.int32), (blk_e[1:] != blk_e[:-1]).astype(jnp.int32)])
    n_used = (pad_end[-1:] // MOE_ROWS).astype(jnp.int32)
    x_ext = jnp.concatenate([xt.astype(BF16), jnp.zeros((1, xt.shape[-1]), BF16)], axis=0)
    x_rows = x_ext[row_tok]
    y_rows = _moe_experts(x_rows, blk_e, blk_first, n_used, w1, b1, w2, b2, layer)
    y_sel = y_rows[dest].reshape(n_tok, TOP_K, xt.shape[-1])
    return jnp.einsum('tkd,tk->td', y_sel, gates.astype(y_sel.dtype),
                      precision=lax.Precision.HIGHEST)


def kernel(x_prompt, x_sample, state_rwkv_shift, state_rwkv_wkv, state_gdn_conv, state_gdn_delta, state_ret, ln_gain, ln_bias, rwkv_mu, rwkv_w_rkv, rwkv_w_o, rwkv_w0, rwkv_w1, rwkv_w2, rwkv_a0, rwkv_a1, rwkv_a2, rwkv_g1, rwkv_g2, rwkv_k_k, rwkv_k_a, rwkv_r_k, rwkv_lnx_g, rwkv_lnx_b, rwkv_v0, rwkv_v1, rwkv_v2, gdn_w_in, gdn_conv_w, gdn_a_log, gdn_dt_bias, gdn_norm_w, gdn_w_o, ret_w_in, ret_w_o, moe_w_router, moe_b_router, moe_w1, moe_b1, moe_w2, moe_b2):
    n_p = x_prompt.shape[0]
    n_ptok = x_prompt.shape[0] * x_prompt.shape[1]
    hp, hs = x_prompt, x_sample
    vfirst_p = None
    vfirst_s = None
    p_shift, p_wkv, p_conv, p_delta, p_ret = [], [], [], [], []
    s_shift, s_wkv, s_conv, s_delta, s_ret = [], [], [], [], []
    for i in range(DEPTH):
        kind, j = i % N_MIXERS, i // N_MIXERS
        if kind == 0:
            prm = (rwkv_mu[j], rwkv_w_rkv[j], rwkv_w_o[j], rwkv_w0[j], rwkv_w1[j], rwkv_w2[j],
                   rwkv_a0[j], rwkv_a1[j], rwkv_a2[j], rwkv_g1[j], rwkv_g2[j], rwkv_k_k[j],
                   rwkv_k_a[j], rwkv_r_k[j], rwkv_lnx_g[j], rwkv_lnx_b[j])
            v_res = None if j == 0 else (rwkv_v0[j - 1], rwkv_v1[j - 1], rwkv_v2[j - 1])
            mp, sh_p, st_p, vfirst_p = rwkv7_time_mix(
                hp, jnp.zeros((n_p, D_MODEL), hp.dtype), None, vfirst_p, v_res, *prm)
            ms, sh_s, st_s, vfirst_s = rwkv7_time_mix(
                hs, state_rwkv_shift[j], state_rwkv_wkv[j], vfirst_s, v_res, *prm)
            p_shift.append(sh_p)
            p_wkv.append(st_p)
            s_shift.append(sh_s)
            s_wkv.append(st_s)
        elif kind == 1:
            prm = (gdn_w_in[j], gdn_conv_w[j], gdn_a_log[j], gdn_dt_bias[j], gdn_norm_w[j], gdn_w_o[j])
            mp, cv_p, st_p = gated_deltanet_mix(hp, None, None, *prm)
            ms, cv_s, st_s = gated_deltanet_mix(hs, state_gdn_conv[j], state_gdn_delta[j], *prm)
            p_conv.append(cv_p)
            p_delta.append(st_p)
            s_conv.append(cv_s)
            s_delta.append(st_s)
        else:
            mp, st_p = retention_mix(hp, None, 0, ret_w_in[j], ret_w_o[j])
            ms, st_s = retention_mix(hs, state_ret[j], PAST_LEN, ret_w_in[j], ret_w_o[j])
            p_ret.append(st_p)
            s_ret.append(st_s)
        hp = layer_norm(DEEPNORM_ALPHA * hp + mp, ln_gain[i, 0], ln_bias[i, 0])
        hs = layer_norm(DEEPNORM_ALPHA * hs + ms, ln_gain[i, 0], ln_bias[i, 0])
        xt = jnp.concatenate([hp.reshape(-1, D_MODEL), hs.reshape(-1, D_MODEL)], axis=0)
        yt = routed_moe(xt, moe_w_router, moe_b_router, moe_w1, moe_b1, moe_w2, moe_b2, i)
        xt = layer_norm(DEEPNORM_ALPHA * xt + yt, ln_gain[i, 1], ln_bias[i, 1])
        hp = xt[:n_ptok].reshape(x_prompt.shape)
        hs = xt[n_ptok:].reshape(x_sample.shape)
    stk = lambda rows, like: jnp.stack(rows).astype(like.dtype)
    return (hp, hs,
            stk(p_shift, state_rwkv_shift), stk(p_wkv, state_rwkv_wkv),
            stk(p_conv, state_gdn_conv), stk(p_delta, state_gdn_delta), stk(p_ret, state_ret),
            stk(s_shift, state_rwkv_shift), stk(s_wkv, state_rwkv_wkv),
            stk(s_conv, state_gdn_conv), stk(s_delta, state_gdn_delta), stk(s_ret, state_ret))
```

```python
import functools
import math

import jax
import jax.numpy as jnp
from jax import lax
from jax.experimental import pallas as pl
from jax.experimental.pallas import tpu as pltpu

F32 = jnp.float32
BF16 = jnp.bfloat16

D_MODEL = 4096
DEPTH = 4
PAST_LEN = 16384
N_MIXERS = 3

DEEPNORM_ALPHA = (2.0 * DEPTH) ** 0.25
LN_EPS = 1e-5

A_HEAD = 64
A_HEADS = D_MODEL // A_HEAD
A_GN_EPS = 64e-5

B_DK = 128
B_HEADS = D_MODEL // B_DK
B_DV = 2 * B_DK
B_QK = B_HEADS * B_DK
B_V = B_HEADS * B_DV
B_CONV = 4
B_CONV_CH = 2 * B_QK + B_V
B_CHUNK = 64

C_DK = 256
C_HEADS = D_MODEL // C_DK
C_DV = 2 * C_DK
C_QK = C_HEADS * C_DK
C_V = C_HEADS * C_DV
C_CHUNK = 128
ROT_BASE = 10000.0

N_EXPERTS = 32
TOP_K = 4
D_EXPERT = D_MODEL // 4
SWIGLU_ALPHA = 1.702
SWIGLU_LIMIT = 7.0

LANES = 128

MOE_ROWS = 256
MOE_TN1 = 1024
MOE_TN2 = 2048
LANE_PAIR_GROUP = 256
VMEM_LIMIT = 56 * 1024 * 1024

WKV_CHUNK = 64
WKV_PAIRS = 4
GDN_HEADS = 8
RET_HEADS = 4


def layer_norm(x, g, b):
    xf = x.astype(F32)
    xc = xf - xf.mean(-1, keepdims=True)
    var = jnp.mean(xc * xc, -1, keepdims=True)
    y = xc * lax.rsqrt(var + LN_EPS) * g.astype(F32) + b.astype(F32)
    return y.astype(x.dtype)


def head_norm(x, eps):
    xc = x - x.mean(-1, keepdims=True)
    return xc * lax.rsqrt(jnp.mean(xc * xc, -1, keepdims=True) + eps)


def rms_norm(x, eps):
    return x * lax.rsqrt(jnp.mean(x * x, -1, keepdims=True) + eps)


def l2_normalize(x, eps):
    return x * lax.rsqrt(jnp.sum(x * x, -1, keepdims=True) + eps)


def _dot(a, b):
    return jnp.dot(a.astype(BF16), b.astype(BF16), preferred_element_type=F32)


def _dot_nt(a, b):
    return lax.dot_general(a.astype(BF16), b.astype(BF16), (((1,), (1,)), ((), ())),
                           preferred_element_type=F32)


def _dot_tn(a, b):
    return lax.dot_general(a.astype(BF16), b.astype(BF16), (((0,), (0,)), ((), ())),
                           preferred_element_type=F32)


def _split3(x):
    h1 = x.astype(BF16)
    r1 = x - h1.astype(F32)
    h2 = r1.astype(BF16)
    h3 = (r1 - h2.astype(F32)).astype(BF16)
    return h1, h2, h3


def _cumsum_rows(tril_bf, x):
    h1, h2, h3 = _split3(x)
    d = lambda y: jnp.dot(tril_bf, y, preferred_element_type=F32)
    return d(h1) + (d(h2) + d(h3))


def _level_masks(c):
    row = lax.broadcasted_iota(jnp.int32, (c, c), 0)
    col = lax.broadcasted_iota(jnp.int32, (c, c), 1)
    masks = []
    size = 1
    while size < c:
        same_big = (row // (2 * size)) == (col // (2 * size))
        same_small = (row // size) == (col // size)
        masks.append(same_big & jnp.logical_not(same_small) & (row > col))
        size *= 2
    return masks


def _unit_lower_inverses(a_strict, eye, masks):
    n = range(len(a_strict))
    t = [eye + jnp.where(masks[0], a_strict[i], 0.0) for i in n]
    for m in masks[1:]:
        tn = [_dot(t[i], jnp.where(m, a_strict[i], 0.0)) for i in n]
        t = [t[i] + _dot(tn[i], t[i]) for i in n]
    return t


def _head_sums(x, m0):
    zero = jnp.zeros_like(x)
    s_lo = jnp.sum(jnp.where(m0, x, zero), axis=1, keepdims=True)
    s_hi = jnp.sum(jnp.where(m0, zero, x), axis=1, keepdims=True)
    return jnp.where(m0, s_lo, s_hi)


def _softplus(x):
    return jnp.maximum(x, 0.0) + jnp.log(1.0 + jnp.exp(-jnp.abs(x)))


def _wkv_kernel(*refs, has_vres):
    if has_vres:
        (r_ref, k_ref, v_ref, wp_ref, ap_ref, g_ref, vp_ref, vf_ref,
         kkw_ref, ka_ref, rk_ref, lg_ref, lb_ref, y_ref, s_out_ref, s_scr) = refs
    else:
        (r_ref, k_ref, v_ref, wp_ref, ap_ref, g_ref,
         kkw_ref, ka_ref, rk_ref, lg_ref, lb_ref, y_ref, s_out_ref, s_scr) = refs
    c_idx = pl.program_id(2)
    n_c = pl.num_programs(2)
    c = r_ref.shape[0]
    n_pairs = r_ref.shape[1] // LANES
    pairs = range(n_pairs)
    heads = [(p, h) for p in pairs for h in range(2)]

    @pl.when(c_idx == 0)
    def _():
        s_scr[...] = jnp.zeros_like(s_scr)

    row = lax.broadcasted_iota(jnp.int32, (c, c), 0)
    col = lax.broadcasted_iota(jnp.int32, (c, c), 1)
    incl = row >= col
    strict = row > col
    tril_bf = incl.astype(BF16)
    eye = (row == col).astype(F32)
    lane = lax.broadcasted_iota(jnp.int32, (1, LANES), 1)
    m0 = lane < A_HEAD
    bd = (lax.broadcasted_iota(jnp.int32, (LANES, LANES), 0) // A_HEAD
          == lax.broadcasted_iota(jnp.int32, (LANES, LANES), 1) // A_HEAD)
    masks = _level_masks(c)

    sl = [slice(p * LANES, (p + 1) * LANES) for p in pairs]
    r = [r_ref[:, sl[p]] for p in pairs]
    k_raw = [k_ref[:, sl[p]] for p in pairs]
    v = [v_ref[:, sl[p]] for p in pairs]
    s0 = [s_scr[p] for p in pairs]
    lw = [-jnp.exp(-_softplus(-wp_ref[:, sl[p]]) - 0.5) for p in pairs]
    a = [jax.nn.sigmoid(ap_ref[:, sl[p]]) for p in pairs]
    if has_vres:
        v = [v[p] + (vf_ref[:, sl[p]] - v[p]) * jax.nn.sigmoid(vp_ref[:, sl[p]]) for p in pairs]
    kk_raw = [k_raw[p] * kkw_ref[:, sl[p]] for p in pairs]
    kk = [kk_raw[p] * lax.rsqrt(_head_sums(kk_raw[p] * kk_raw[p], m0) + 1e-12) for p in pairs]
    k = [k_raw[p] * (1.0 + (a[p] - 1.0) * ka_ref[:, sl[p]]) for p in pairs]

    cum = [_cumsum_rows(tril_bf, lw[p]) for p in pairs]
    pw = [jnp.exp(cum[p]) for p in pairs]
    inv_pw = [jnp.exp(-cum[p]) for p in pairs]
    pw_last = [pw[p][c - 1:c, :] for p in pairs]
    at = [-kk[p] * jnp.exp(cum[p] - lw[p]) for p in pairs]
    bt = [kk[p] * a[p] * inv_pw[p] for p in pairs]
    kt = [k[p] * inv_pw[p] for p in pairs]
    rt = [r[p] * pw[p] for p in pairs]

    def gram_lhs(p):
        zero = jnp.zeros_like(at[p])
        return jnp.concatenate([jnp.where(m0, at[p], zero), jnp.where(m0, zero, at[p]),
                                jnp.where(m0, rt[p], zero), jnp.where(m0, zero, rt[p])], axis=0)
    lhs = [gram_lhs(p) for p in pairs]
    g_b = [_dot_nt(lhs[p], bt[p]) for p in pairs]
    g_k = [_dot_nt(lhs[p], kt[p]) for p in pairs]
    a_ab = [jnp.where(strict, g_b[p][h * c:(h + 1) * c], 0.0) for p, h in heads]
    a_ak = [jnp.where(strict, g_k[p][h * c:(h + 1) * c], 0.0) for p, h in heads]
    a_rb = [jnp.where(incl, g_b[p][(2 + h) * c:(3 + h) * c], 0.0) for p, h in heads]
    a_rk = [jnp.where(incl, g_k[p][(2 + h) * c:(3 + h) * c], 0.0) for p, h in heads]

    akv = [_dot(a_ak[i], v[p]) for i, (p, h) in enumerate(heads)]
    y0 = [_dot(a_rk[i], v[p]) for i, (p, h) in enumerate(heads)]
    t = _unit_lower_inverses(a_ab, eye, masks)
    wu = [_dot(t[i], jnp.concatenate([at[p], akv[i]], axis=1)) for i, (p, h) in enumerate(heads)]
    w = [jnp.where(m0, wu[2 * p][:, :LANES], wu[2 * p + 1][:, :LANES]) for p in pairs]
    u0 = [jnp.where(m0, wu[2 * p][:, LANES:], wu[2 * p + 1][:, LANES:]) for p in pairs]

    wr = [_dot_nt(jnp.concatenate([w[p], rt[p]], axis=0), s0[p]) for p in pairs]
    u = [wr[p][:c] + u0[p] for p in pairs]
    aru = [_dot(a_rb[i], u[p]) for i, (p, h) in enumerate(heads)]
    upd = [_dot_tn(jnp.concatenate([u[p], v[p]], axis=0),
                   jnp.concatenate([bt[p] * pw_last[p], kt[p] * pw_last[p]], axis=0)) for p in pairs]
    inv_n = 1.0 / A_HEAD
    for p in pairs:
        y = (wr[p][c:] + jnp.where(m0, y0[2 * p], y0[2 * p + 1])
             + jnp.where(m0, aru[2 * p], aru[2 * p + 1]))
        yc = y - _head_sums(y, m0) * inv_n
        yn = yc * lax.rsqrt(_head_sums(yc * yc, m0) * inv_n + A_GN_EPS)
        yn = yn * lg_ref[:, sl[p]] + lb_ref[:, sl[p]]
        bonus = _head_sums(r[p] * k[p] * rk_ref[:, sl[p]], m0) * v[p]
        y_ref[:, sl[p]] = ((yn + bonus) * g_ref[:, sl[p]]).astype(y_ref.dtype)
        s_scr[p] = s0[p] * pw_last[p] + jnp.where(bd, upd[p], 0.0)

    @pl.when(c_idx == n_c - 1)
    def _():
        for p in pairs:
            s_fin = s_scr[p]
            s_out_ref[2 * p] = s_fin[:A_HEAD, :A_HEAD]
            s_out_ref[2 * p + 1] = s_fin[A_HEAD:, A_HEAD:]


def wkv7_chunked(r, k, v, w_pre, a_pre, g, vres, k_k, k_a, r_k, lnx_g, lnx_b):
    nb, t_len, d = r.shape
    lanes = LANES * WKV_PAIRS
    grid = (nb, d // lanes, t_len // WKV_CHUNK)
    blk = pl.BlockSpec((None, WKV_CHUNK, lanes), lambda b, g, c: (b, c, g))
    prm = pl.BlockSpec((1, lanes), lambda b, g, c: (0, g))
    acts = [r, k, v, w_pre, a_pre, g] + (list(vres) if vres is not None else [])
    prms = [t.astype(F32).reshape(1, d) for t in (k_k, k_a, r_k, lnx_g, lnx_b)]
    return pl.pallas_call(
        functools.partial(_wkv_kernel, has_vres=vres is not None),
        grid=grid,
        in_specs=[blk] * len(acts) + [prm] * len(prms),
        out_specs=[blk, pl.BlockSpec((None, 2 * WKV_PAIRS, A_HEAD, A_HEAD), lambda b, g, c: (b, g, 0, 0))],
        out_shape=[jax.ShapeDtypeStruct((nb, t_len, d), BF16),
                   jax.ShapeDtypeStruct((nb, d // A_HEAD, A_HEAD, A_HEAD), F32)],
        scratch_shapes=[pltpu.VMEM((WKV_PAIRS, LANES, LANES), F32)],
        compiler_params=pltpu.CompilerParams(dimension_semantics=("parallel", "parallel", "arbitrary")),
        name="wkv7_chunked",
    )(*acts, *prms)


MM_TM = 1024
MM_TN = 512
MIX_TM = 512
MIX_TN = 1024
LN_TM = 256


def _mm_kernel(a_ref, w_ref, b_ref, o_ref, wbf_ref, *, act):
    @pl.when(pl.program_id(1) == 0)
    def _():
        wbf_ref[...] = w_ref[...].astype(BF16)

    acc = jnp.dot(a_ref[...].astype(BF16), wbf_ref[...], preferred_element_type=F32) + b_ref[...]
    if act == "tanh":
        acc = jnp.tanh(acc)
    elif act == "sigmoid":
        acc = jax.nn.sigmoid(acc)
    o_ref[...] = acc.astype(o_ref.dtype)


def dense(a, w, w_index=(), bias=None, act=None, out_dtype=F32):
    m, kdim = a.shape
    n = w.shape[-1]
    tm = min(MM_TM, m)
    tn = min(MM_TN, n)
    assert m % tm == 0 and n % tn == 0 and w.shape[-2] == kdim
    lead = tuple(w_index)
    b2 = (jnp.zeros((n,), F32) if bias is None else bias.astype(F32)).reshape(1, n)
    return pl.pallas_call(
        functools.partial(_mm_kernel, act=act),
        grid=(n // tn, m // tm),
        in_specs=[pl.BlockSpec((tm, kdim), lambda j, i: (i, 0)),
                  pl.BlockSpec((None,) * len(lead) + (kdim, tn), lambda j, i: lead + (0, j)),
                  pl.BlockSpec((1, tn), lambda j, i: (0, j))],
        out_specs=pl.BlockSpec((tm, tn), lambda j, i: (i, j)),
        out_shape=jax.ShapeDtypeStruct((m, n), out_dtype),
        scratch_shapes=[pltpu.VMEM((kdim, tn), BF16)],
        compiler_params=pltpu.CompilerParams(dimension_semantics=("arbitrary", "arbitrary"),
                                             vmem_limit_bytes=VMEM_LIMIT),
        name="dense",
    )(a, w, b2)


def _mix_kernel(x_ref, xp_ref, mu_ref, *o_refs):
    x = x_ref[...]
    xx = xp_ref[...] - x
    for j, o_ref in enumerate(o_refs):
        o_ref[...] = (x + xx * mu_ref[j]).astype(o_ref.dtype)


def token_shift_mixes(x, x_prev, mu):
    m, d = x.shape
    n_mix = mu.shape[0]
    blk = pl.BlockSpec((MIX_TM, MIX_TN), lambda i, j: (i, j))
    return pl.pallas_call(
        _mix_kernel,
        grid=(m // MIX_TM, d // MIX_TN),
        in_specs=[blk, blk, pl.BlockSpec((n_mix, 1, MIX_TN), lambda i, j: (0, 0, j))],
        out_specs=[blk] * n_mix,
        out_shape=[jax.ShapeDtypeStruct((m, d), BF16)] * n_mix,
        compiler_params=pltpu.CompilerParams(dimension_semantics=("parallel", "parallel")),
        name="token_shift_mixes",
    )(x, x_prev, mu.astype(F32).reshape(n_mix, 1, d))


def _res_ln_kernel(x_ref, m_ref, g_ref, b_ref, o_ref):
    h = DEEPNORM_ALPHA * x_ref[...] + m_ref[...]
    hc = h - jnp.mean(h, axis=-1, keepdims=True)
    var = jnp.mean(hc * hc, axis=-1, keepdims=True)
    o_ref[...] = hc * lax.rsqrt(var + LN_EPS) * g_ref[...] + b_ref[...]


def residual_layer_norm(x, m, g, b):
    rows, d = x.shape
    blk = pl.BlockSpec((LN_TM, d), lambda i: (i, 0))
    prm = pl.BlockSpec((1, d), lambda i: (0, 0))
    return pl.pallas_call(
        _res_ln_kernel,
        grid=(rows // LN_TM,),
        in_specs=[blk, blk, prm, prm],
        out_specs=blk,
        out_shape=jax.ShapeDtypeStruct((rows, d), F32),
        compiler_params=pltpu.CompilerParams(dimension_semantics=("parallel",),
                                             vmem_limit_bytes=VMEM_LIMIT),
        name="residual_layer_norm",
    )(x, m, g.astype(F32).reshape(1, d), b.astype(F32).reshape(1, d))


def rwkv7_time_mix_prompt(x, j, vfirst, mu, w_rkv, w_o, w0, w1, w2, a0, a1, a2, g1, g2,
                          k_k, k_a, r_k, lnx_g, lnx_b, v012):
    nb, t_len, d = x.shape
    x_prev = jnp.concatenate([jnp.zeros((nb, 1, d), x.dtype), x[:, :-1]], axis=1)
    m = token_shift_mixes(x.reshape(-1, d), x_prev.reshape(-1, d), mu[j])
    r = dense(m[0], w_rkv, (j, 0))
    k = dense(m[1], w_rkv, (j, 1))
    v = dense(m[2], w_rkv, (j, 2))
    w_pre = dense(dense(m[3], w1, (j,), act="tanh", out_dtype=BF16), w2, (j,), bias=w0[j])
    a_pre = dense(dense(m[4], a1, (j,), out_dtype=BF16), a2, (j,), bias=a0[j])
    g = dense(dense(m[5], g1, (j,), act="sigmoid", out_dtype=BF16), g2, (j,))
    b3 = lambda t: t.reshape(nb, t_len, d)
    if vfirst is None:
        vres = None
        vfirst = b3(v)
    else:
        v0, v1, v2 = v012
        vres = (b3(dense(dense(m[2], v1, (j - 1,), out_dtype=BF16), v2, (j - 1,), bias=v0[j - 1])), vfirst)
    y, wkv = wkv7_chunked(b3(r), b3(k), b3(v), b3(w_pre), b3(a_pre), b3(g), vres,
                          k_k[j], k_a[j], r_k[j].reshape(d), lnx_g[j], lnx_b[j])
    out = dense(y.reshape(-1, d), w_o, (j,))
    return b3(out), x[:, -1], wkv, vfirst


def wkv7_scan(r, w, k, v, kk, a, s0):
    def step(s, inp):
        r_t, w_t, k_t, v_t, kk_t, a_t = inp
        sa = jnp.einsum('bhij,bhj->bhi', s, -kk_t)
        s = (s * w_t[:, :, None, :] + sa[..., None] * (kk_t * a_t)[:, :, None, :]
             + v_t[..., None] * k_t[:, :, None, :])
        return s, jnp.einsum('bhij,bhj->bhi', s, r_t)
    xs = tuple(jnp.moveaxis(t, 1, 0) for t in (r, w, k, v, kk, a))
    s, y = lax.scan(step, s0, xs)
    return jnp.moveaxis(y, 0, 1), s


def rwkv7_time_mix(x, shift0, wkv0, v_first, v_res, mu, w_rkv, w_o, w0, w1, w2,
                   a0, a1, a2, g1, g2, k_k, k_a, r_k, lnx_g, lnx_b):
    nb, t_len, d = x.shape
    x_prev = jnp.concatenate([shift0[:, None, :].astype(x.dtype), x[:, :-1]], axis=1)
    xx = x_prev - x
    mix = lambda j: x + xx * mu[j]
    r = mix(0) @ w_rkv[0]
    k = mix(1) @ w_rkv[1]
    xv = mix(2)
    v = xv @ w_rkv[2]
    w_log = -jax.nn.softplus(-(w0 + jnp.tanh(mix(3) @ w1) @ w2).astype(F32)) - 0.5
    a = jax.nn.sigmoid((a0 + (mix(4) @ a1) @ a2).astype(F32))
    g = jax.nn.sigmoid(mix(5) @ g1) @ g2
    if v_first is None:
        v_first = v
    else:
        v0, v1, v2 = v_res
        v = v + (v_first - v) * jax.nn.sigmoid(v0 + (xv @ v1) @ v2)
    hs = lambda t: t.astype(F32).reshape(nb, t_len, A_HEADS, A_HEAD)
    r, k, v, a = hs(r), hs(k), hs(v), hs(a)
    kk = l2_normalize(k * k_k.astype(F32).reshape(A_HEADS, A_HEAD), 1e-12)
    k = k * (1.0 + (a - 1.0) * k_a.astype(F32).reshape(A_HEADS, A_HEAD))
    decay = hs(jnp.exp(-jnp.exp(w_log)))
    y, wkv = wkv7_scan(r, decay, k, v, kk, a, wkv0.astype(F32))
    y = head_norm(y, A_GN_EPS).reshape(nb, t_len, d) * lnx_g.astype(F32) + lnx_b.astype(F32)
    bonus = jnp.sum(r * k * r_k.astype(F32), -1, keepdims=True) * v
    y = (y + bonus.reshape(nb, t_len, d)) * g.astype(F32)
    return y.astype(x.dtype) @ w_o, x[:, -1], wkv, v_first


def _gdn_kernel(q_ref, k_ref, v_ref, z_ref, beta_ref, g_ref, nw_ref, o_ref, s_out_ref, s_scr):
    c_idx = pl.program_id(2)
    n_c = pl.num_programs(2)
    c = q_ref.shape[0]
    nh = q_ref.shape[1] // B_DK
    hs = range(nh)

    @pl.when(c_idx == 0)
    def _():
        s_scr[...] = jnp.zeros_like(s_scr)

    row = lax.broadcasted_iota(jnp.int32, (c, c), 0)
    col = lax.broadcasted_iota(jnp.int32, (c, c), 1)
    incl = row >= col
    strict = row > col
    tril_bf = incl.astype(BF16)
    triu_bf = (row <= col).astype(BF16)
    eye = (row == col).astype(F32)
    masks = _level_masks(c)

    g = g_ref[...]
    beta = beta_ref[...]
    gc = _cumsum_rows(tril_bf, g)
    g3 = _split3(g)
    tn = lambda x: lax.dot_general(x, triu_bf, (((0,), (0,)), ((), ())), preferred_element_type=F32)
    gc_t = tn(g3[0]) + (tn(g3[1]) + tn(g3[2]))

    q = [q_ref[:, h * B_DK:(h + 1) * B_DK] for h in hs]
    k = [k_ref[:, h * B_DK:(h + 1) * B_DK] for h in hs]
    v = [v_ref[:, h * B_DV:(h + 1) * B_DV] for h in hs]
    z = [z_ref[:, h * B_DV:(h + 1) * B_DV] for h in hs]
    s0 = [s_scr[h] for h in hs]
    nw = nw_ref[...]

    gcol = [gc[:, h:h + 1] for h in hs]
    bcol = [beta[:, h:h + 1] for h in hs]
    glast = [gc[c - 1:c, h:h + 1] for h in hs]
    decay = [jnp.where(incl, jnp.exp(jnp.where(incl, gcol[h] - gc_t[h:h + 1, :], 0.0)), 0.0) for h in hs]
    egc = [jnp.exp(gcol[h]) for h in hs]
    kb = [k[h] * bcol[h] for h in hs]
    gram = [_dot_nt(jnp.concatenate([kb[h], q[h]], axis=0), k[h]) for h in hs]
    a_neg = [jnp.where(strict, -gram[h][:c] * decay[h], 0.0) for h in hs]
    qk = [gram[h][c:] * decay[h] for h in hs]
    t = _unit_lower_inverses(a_neg, eye, masks)
    sol = [_dot(t[h], jnp.concatenate([v[h] * bcol[h], kb[h] * egc[h]], axis=1)) for h in hs]
    wq = [_dot(jnp.concatenate([sol[h][:, B_DV:], q[h] * egc[h]], axis=0), s0[h]) for h in hs]
    v_new = [sol[h][:, :B_DV] - wq[h][:c] for h in hs]
    o = [wq[h][c:] + _dot(qk[h], v_new[h]) for h in hs]
    upd = [_dot_tn(k[h] * jnp.exp(glast[h] - gcol[h]), v_new[h]) for h in hs]
    for h in hs:
        on = o[h] * lax.rsqrt(jnp.mean(o[h] * o[h], axis=-1, keepdims=True) + 1e-6) * nw
        o_ref[:, h * B_DV:(h + 1) * B_DV] = on * (z[h] * jax.nn.sigmoid(z[h]))
        s_scr[h] = s0[h] * jnp.exp(glast[h]) + upd[h]

    @pl.when(c_idx == n_c - 1)
    def _():
        s_out_ref[...] = s_scr[...]


def gdn_chunked(q, k, v, proj, beta_g, g_g, norm_w):
    nb, t_len, _ = q.shape
    n_groups = B_HEADS // GDN_HEADS
    grid = (nb, n_groups, t_len // B_CHUNK)
    v_width = GDN_HEADS * B_DV
    z_first = B_CONV_CH // v_width
    qk_blk = pl.BlockSpec((None, B_CHUNK, GDN_HEADS * B_DK), lambda b, g, c: (b, c, g))
    v_blk = pl.BlockSpec((None, B_CHUNK, v_width), lambda b, g, c: (b, c, g))
    z_blk = pl.BlockSpec((None, B_CHUNK, v_width), lambda b, g, c: (b, c, z_first + g))
    sc_blk = pl.BlockSpec((None, None, B_CHUNK, GDN_HEADS), lambda b, g, c: (b, g, c, 0))
    return pl.pallas_call(
        _gdn_kernel,
        grid=grid,
        in_specs=[qk_blk, qk_blk, v_blk, z_blk, sc_blk, sc_blk,
                  pl.BlockSpec((1, B_DV), lambda b, g, c: (0, 0))],
        out_specs=[v_blk, pl.BlockSpec((None, GDN_HEADS, B_DK, B_DV), lambda b, g, c: (b, g, 0, 0))],
        out_shape=[jax.ShapeDtypeStruct((nb, t_len, B_V), F32),
                   jax.ShapeDtypeStruct((nb, B_HEADS, B_DK, B_DV), F32)],
        scratch_shapes=[pltpu.VMEM((GDN_HEADS, B_DK, B_DV), F32)],
        compiler_params=pltpu.CompilerParams(dimension_semantics=("parallel", "parallel", "arbitrary")),
        name="gdn_chunked",
    )(q, k, v, proj, beta_g, g_g, norm_w.reshape(1, B_DV))


def gated_delta_chunked(q, k, v, beta, g, s0):
    nb, nh, t_len, dk = q.shape
    dv = v.shape[-1]
    c = math.gcd(t_len, B_CHUNK)
    n = t_len // c
    q, k, v = (t.reshape(nb, nh, n, c, t.shape[-1]) for t in (q, k, v))
    beta, g = (t.reshape(nb, nh, n, c) for t in (beta, g))
    gc = jnp.cumsum(g, axis=-1)
    causal = jnp.tril(jnp.ones((c, c), dtype=bool))
    eye = jnp.eye(c, dtype=jnp.float32)
    diff = gc[..., :, None] - gc[..., None, :]
    decay = jnp.where(causal, jnp.exp(jnp.where(causal, diff, 0.0)), 0.0)
    kb = k * beta[..., None]
    lower = jnp.einsum('bhncd,bhnsd->bhncs', kb, k) * decay * (1.0 - eye)
    rhs = jnp.concatenate([v * beta[..., None], kb * jnp.exp(gc)[..., None]], axis=-1)
    sol = lax.linalg.triangular_solve(lower + eye, rhs, left_side=True, lower=True)
    u, w = sol[..., :dv], sol[..., dv:]
    qk = jnp.einsum('bhncd,bhnsd->bhncs', q, k) * decay
    q_dec = q * jnp.exp(gc)[..., None]
    k_dec = k * jnp.exp(gc[..., -1:] - gc)[..., None]
    g_tot = jnp.exp(gc[..., -1])[..., None, None]

    def step(s, inp):
        u_c, w_c, qk_c, qd_c, kd_c, gt_c = inp
        v_new = u_c - jnp.einsum('bhcd,bhde->bhce', w_c, s)
        o = jnp.einsum('bhcd,bhde->bhce', qd_c, s) + jnp.einsum('bhcs,bhse->bhce', qk_c, v_new)
        s = s * gt_c + jnp.einsum('bhcd,bhce->bhde', kd_c, v_new)
        return s, o
    xs = tuple(jnp.moveaxis(t, 2, 0) for t in (u, w, qk, q_dec, k_dec, g_tot))
    s, o = lax.scan(step, s0, xs)
    return jnp.moveaxis(o, 0, 2).reshape(nb, nh, t_len, dv), s


def gated_deltanet_mix(x, conv0, s0, w_in, conv_w, a_log, dt_bias, norm_w, w_o):
    nb, t_len, _ = x.shape
    proj = x @ w_in
    qkv = proj[..., :B_CONV_CH]
    a_in = proj[..., B_CONV_CH + B_V:B_CONV_CH + B_V + B_HEADS]
    b_in = proj[..., B_CONV_CH + B_V + B_HEADS:]
    if conv0 is None:
        conv0 = jnp.zeros((nb, B_CONV - 1, B_CONV_CH), qkv.dtype)
    xp = jnp.concatenate([conv0.astype(qkv.dtype), qkv], axis=1)
    conv = xp[:, 0:t_len] * conv_w[0]
    for j in range(1, B_CONV):
        conv = conv + xp[:, j:j + t_len] * conv_w[j]
    conv = jax.nn.silu(conv.astype(F32))
    new_conv = xp[:, t_len:]
    q = conv[..., :B_QK].reshape(nb, t_len, B_HEADS, B_DK)
    k = conv[..., B_QK:2 * B_QK].reshape(nb, t_len, B_HEADS, B_DK)
    v = conv[..., 2 * B_QK:].reshape(nb, t_len, B_HEADS, B_DV)
    q = l2_normalize(q, 1e-6) * (B_DK ** -0.5)
    k = l2_normalize(k, 1e-6)
    beta = jax.nn.sigmoid(b_in.astype(F32))
    g = -jnp.exp(a_log.astype(F32)) * jax.nn.softplus(a_in.astype(F32) + dt_bias.astype(F32))
    if s0 is None:
        grp = lambda t: jnp.transpose(t.reshape(nb, t_len, B_HEADS // GDN_HEADS, GDN_HEADS), (0, 2, 1, 3))
        o, s = gdn_chunked(q.reshape(nb, t_len, B_QK), k.reshape(nb, t_len, B_QK),
                           v.reshape(nb, t_len, B_V), proj, grp(beta), grp(g), norm_w.astype(F32))
        return o.astype(x.dtype) @ w_o, new_conv, s
    z = proj[..., B_CONV_CH:B_CONV_CH + B_V]
    tr = lambda t: jnp.swapaxes(t, 1, 2)
    o, s = gated_delta_chunked(tr(q), tr(k), tr(v), tr(beta), tr(g), s0.astype(F32))
    o = rms_norm(tr(o), 1e-6) * norm_w.astype(F32)
    o = o * jax.nn.silu(z.astype(F32).reshape(nb, t_len, B_HEADS, B_DV))
    return o.reshape(nb, t_len, B_V).astype(x.dtype) @ w_o, new_conv, s


def _ret_kernel(lg_ref, q_ref, k_ref, v_ref, gate_ref, cos_ref, sin_ref, o_ref, s_out_ref, s_scr):
    g_idx = pl.program_id(1)
    c_idx = pl.program_id(2)
    n_c = pl.num_programs(2)
    c = q_ref.shape[0]
    nh = q_ref.shape[1] // C_DK
    hs = range(nh)

    @pl.when(c_idx == 0)
    def _():
        s_scr[...] = jnp.zeros_like(s_scr)

    width = q_ref.shape[1]
    lane = lax.broadcasted_iota(jnp.int32, (1, width), 1)
    even = (lane % 2) == 0
    cos = jnp.concatenate([cos_ref[...]] * nh, axis=1)
    sin = jnp.concatenate([sin_ref[...]] * nh, axis=1)

    def rotate(x):
        partner = jnp.where(even, pltpu.roll(x, width - 1, axis=1), pltpu.roll(x, 1, axis=1))
        return x * cos + partner * sin

    qr = rotate(q_ref[...])
    kr = rotate(k_ref[...]) * (C_DK ** -0.5)
    row = lax.broadcasted_iota(jnp.int32, (c, c), 0)
    col = lax.broadcasted_iota(jnp.int32, (c, c), 1)
    rel = (row - col).astype(F32)
    causal = row >= col
    idx = lax.broadcasted_iota(jnp.int32, (c, 1), 0).astype(F32)

    lg = [lg_ref[g_idx * nh + h] for h in hs]
    q = [qr[:, h * C_DK:(h + 1) * C_DK] for h in hs]
    k = [kr[:, h * C_DK:(h + 1) * C_DK] for h in hs]
    v = [v_ref[:, h * C_DV:(h + 1) * C_DV] for h in hs]
    s0 = [s_scr[h] for h in hs]
    dmat = [jnp.where(causal, jnp.exp(jnp.where(causal, rel, 0.0) * lg[h]), 0.0) for h in hs]
    inner = [_dot_nt(q[h], k[h]) * dmat[h] for h in hs]
    o = [_dot(inner[h], v[h]) + _dot(q[h] * jnp.exp((idx + 1.0) * lg[h]), s0[h]) for h in hs]
    upd = [_dot_tn(k[h] * jnp.exp((c - 1.0 - idx) * lg[h]), v[h]) for h in hs]
    for h in hs:
        oc = o[h] - jnp.mean(o[h], axis=-1, keepdims=True)
        on = oc * lax.rsqrt(jnp.mean(oc * oc, axis=-1, keepdims=True) + 1e-5)
        gt = gate_ref[:, h * C_DV:(h + 1) * C_DV]
        o_ref[:, h * C_DV:(h + 1) * C_DV] = gt * jax.nn.sigmoid(gt) * on
        s_scr[h] = s0[h] * jnp.exp(c * lg[h]) + upd[h]

    @pl.when(c_idx == n_c - 1)
    def _():
        s_out_ref[...] = s_scr[...]


def retention_chunked_pallas(proj, cos, sin_signed, log_gamma):
    nb, t_len, _ = proj.shape
    n_groups = C_HEADS // RET_HEADS
    grid = (nb, n_groups, t_len // C_CHUNK)
    qk_width = RET_HEADS * C_DK
    v_width = RET_HEADS * C_DV
    k_first = C_QK // qk_width
    v_first = 2 * C_QK // v_width
    gate_first = (2 * C_QK + C_V) // v_width
    q_blk = pl.BlockSpec((None, C_CHUNK, qk_width), lambda b, g, c, lg: (b, c, g))
    k_blk = pl.BlockSpec((None, C_CHUNK, qk_width), lambda b, g, c, lg: (b, c, k_first + g))
    v_blk = pl.BlockSpec((None, C_CHUNK, v_width), lambda b, g, c, lg: (b, c, v_first + g))
    gate_blk = pl.BlockSpec((None, C_CHUNK, v_width), lambda b, g, c, lg: (b, c, gate_first + g))
    o_blk = pl.BlockSpec((None, C_CHUNK, v_width), lambda b, g, c, lg: (b, c, g))
    tab_blk = pl.BlockSpec((C_CHUNK, C_DK), lambda b, g, c, lg: (c, 0))
    return pl.pallas_call(
        _ret_kernel,
        grid_spec=pltpu.PrefetchScalarGridSpec(
            num_scalar_prefetch=1,
            grid=grid,
            in_specs=[q_blk, k_blk, v_blk, gate_blk, tab_blk, tab_blk],
            out_specs=[o_blk, pl.BlockSpec((None, RET_HEADS, C_DK, C_DV), lambda b, g, c, lg: (b, g, 0, 0))],
            scratch_shapes=[pltpu.VMEM((RET_HEADS, C_DK, C_DV), F32)],
        ),
        out_shape=[jax.ShapeDtypeStruct((nb, t_len, C_V), F32),
                   jax.ShapeDtypeStruct((nb, C_HEADS, C_DK, C_DV), F32)],
        compiler_params=pltpu.CompilerParams(dimension_semantics=("parallel", "parallel", "arbitrary"),
                                             vmem_limit_bytes=VMEM_LIMIT),
        name="retention_chunked",
    )(log_gamma, proj, proj, proj, proj, cos, sin_signed)


def rotary_tables(t_len, start):
    half = C_DK // 2
    inv_freq = 1.0 / (ROT_BASE ** jnp.linspace(0.0, 1.0, half, dtype=jnp.float32))
    pos = start + jnp.arange(t_len, dtype=F32)
    ang = pos[:, None] * inv_freq[None, :]
    sin, cos = jnp.sin(ang), jnp.cos(ang)
    return jnp.repeat(cos, 2, axis=1), jnp.stack([-sin, sin], axis=-1).reshape(t_len, C_DK)


def rel_rotate(x, pos):
    half = x.shape[-1] // 2
    inv_freq = 1.0 / (ROT_BASE ** jnp.linspace(0.0, 1.0, half, dtype=jnp.float32))
    ang = pos[:, None] * inv_freq[None, :]
    sin, cos = jnp.sin(ang), jnp.cos(ang)
    x1, x2 = x[..., 0::2], x[..., 1::2]
    return jnp.stack([x1 * cos - x2 * sin, x2 * cos + x1 * sin], axis=-1).reshape(x.shape)


def retention_chunked(q, k, v, log_gamma, s0):
    nb, nh, t_len, _ = q.shape
    dv = v.shape[-1]
    c = math.gcd(t_len, C_CHUNK)
    n = t_len // c
    q, k, v = (t.reshape(nb, nh, n, c, t.shape[-1]) for t in (q, k, v))
    idx = jnp.arange(c, dtype=jnp.float32)
    lg = log_gamma[:, None]
    rel = idx[:, None] - idx[None, :]
    causal = rel >= 0
    dmat = jnp.where(causal, jnp.exp(jnp.where(causal, rel, 0.0)[None] * lg[..., None]), 0.0)
    inner = jnp.einsum('bhncd,bhnsd->bhncs', q, k) * dmat[:, None]
    o_inner = jnp.einsum('bhncs,bhnse->bhnce', inner, v)
    q_dec = q * jnp.exp((idx + 1.0) * lg)[:, None, :, None]
    k_dec = k * jnp.exp((c - 1.0 - idx) * lg)[:, None, :, None]
    chunk_decay = jnp.exp(c * log_gamma)[:, None, None]

    def step(s, inp):
        qd, kd, vc, oi = inp
        o = oi + jnp.einsum('bhcd,bhde->bhce', qd, s)
        s = s * chunk_decay + jnp.einsum('bhcd,bhce->bhde', kd, vc)
        return s, o
    xs = tuple(jnp.moveaxis(t, 2, 0) for t in (q_dec, k_dec, v, o_inner))
    s, o = lax.scan(step, s0, xs)
    return jnp.moveaxis(o, 0, 2).reshape(nb, nh, t_len, dv), s


def retention_mix(x, s0, start, w_in, w_o):
    nb, t_len, _ = x.shape
    proj = x @ w_in
    log_gamma = jnp.log(1.0 - 2.0 ** (-5.0 - jnp.arange(C_HEADS, dtype=F32)))
    if s0 is None:
        cos, sin_signed = rotary_tables(t_len, start)
        o, s = retention_chunked_pallas(proj.astype(F32), cos, sin_signed, log_gamma)
        return o.astype(x.dtype) @ w_o, s
    heads = lambda t, hd: jnp.swapaxes(t.astype(F32).reshape(nb, t_len, C_HEADS, hd), 1, 2)
    q = heads(proj[..., :C_QK], C_DK)
    k = heads(proj[..., C_QK:2 * C_QK], C_DK)
    v = heads(proj[..., 2 * C_QK:2 * C_QK + C_V], C_DV)
    gate = proj[..., 2 * C_QK + C_V:]
    pos = start + jnp.arange(t_len, dtype=F32)
    q = rel_rotate(q, pos)
    k = rel_rotate(k, pos) * (C_DK ** -0.5)
    o, s = retention_chunked(q, k, v, log_gamma, s0.astype(F32))
    o = head_norm(jnp.swapaxes(o, 1, 2), 1e-5).reshape(nb, t_len, C_V)
    return (jax.nn.silu(gate.astype(F32)) * o).astype(x.dtype) @ w_o, s


def _moe_up_kernel(blk_e_ref, blk_first_ref, n_used_ref, x_ref, w1_ref, b1_ref, sel_ref,
                   o_ref, wbf_ref):
    del blk_e_ref
    i = pl.program_id(1)
    tn = w1_ref.shape[-1]

    @pl.when(i >= n_used_ref[0])
    def _():
        o_ref[...] = jnp.zeros_like(o_ref)

    @pl.when(i < n_used_ref[0])
    def _():
        @pl.when(blk_first_ref[i] == 1)
        def _():
            wbf_ref[...] = w1_ref[...].astype(BF16)

        h = jnp.dot(x_ref[...], wbf_ref[...], preferred_element_type=F32) + b1_ref[...]
        glu = jnp.minimum(h, SWIGLU_LIMIT)
        lin = jnp.clip(pltpu.roll(h, tn - 1, axis=1), -SWIGLU_LIMIT, SWIGLU_LIMIT)
        act = (glu * jax.nn.sigmoid(SWIGLU_ALPHA * glu) * (lin + 1.0)).astype(BF16)
        half = LANE_PAIR_GROUP // 2
        for c in range(tn // LANE_PAIR_GROUP):
            grp = act[:, c * LANE_PAIR_GROUP:(c + 1) * LANE_PAIR_GROUP]
            o_ref[:, c * half:(c + 1) * half] = jnp.dot(
                grp, sel_ref[...], preferred_element_type=F32).astype(o_ref.dtype)


def _moe_down_kernel(blk_e_ref, blk_first_ref, n_used_ref, h_ref, w2_ref, b2_ref, o_ref, wbf_ref):
    del blk_e_ref
    i = pl.program_id(1)

    @pl.when(i >= n_used_ref[0])
    def _():
        o_ref[...] = jnp.zeros_like(o_ref)

    @pl.when(i < n_used_ref[0])
    def _():
        @pl.when(blk_first_ref[i] == 1)
        def _():
            wbf_ref[...] = w2_ref[...].astype(BF16)

        o_ref[...] = jnp.dot(h_ref[...], wbf_ref[...], preferred_element_type=F32) + b2_ref[...]


def _moe_experts(x_rows, blk_e, blk_first, n_used, w1, b1, w2, b2, layer):
    n_rows = x_rows.shape[0]
    n_blocks = n_rows // MOE_ROWS
    d_up = 2 * D_EXPERT
    sel = (jnp.arange(LANE_PAIR_GROUP)[:, None] == 2 * jnp.arange(LANE_PAIR_GROUP // 2)[None, :]).astype(BF16)
    b1r = b1.reshape(DEPTH, N_EXPERTS, 1, d_up)
    b2r = b2.reshape(DEPTH, N_EXPERTS, 1, D_MODEL)
    row_blk = lambda j, i, be, bf, nu: (jnp.minimum(i, nu[0] - 1), 0)
    params = pltpu.CompilerParams(dimension_semantics=("arbitrary", "arbitrary"),
                                  vmem_limit_bytes=VMEM_LIMIT)

    act = pl.pallas_call(
        _moe_up_kernel,
        grid_spec=pltpu.PrefetchScalarGridSpec(
            num_scalar_prefetch=3,
            grid=(d_up // MOE_TN1, n_blocks),
            in_specs=[
                pl.BlockSpec((MOE_ROWS, D_MODEL), row_blk),
                pl.BlockSpec((None, None, D_MODEL, MOE_TN1),
                             lambda j, i, be, bf, nu: (layer, be[i], 0, j)),
                pl.BlockSpec((None, None, 1, MOE_TN1),
                             lambda j, i, be, bf, nu: (layer, be[i], 0, j)),
                pl.BlockSpec((LANE_PAIR_GROUP, LANE_PAIR_GROUP // 2),
                             lambda j, i, be, bf, nu: (0, 0)),
            ],
            out_specs=pl.BlockSpec((MOE_ROWS, MOE_TN1 // 2), lambda j, i, be, bf, nu: (i, j)),
            scratch_shapes=[pltpu.VMEM((D_MODEL, MOE_TN1), BF16)],
        ),
        out_shape=jax.ShapeDtypeStruct((n_rows, D_EXPERT), BF16),
        compiler_params=params,
        name="moe_up",
    )(blk_e, blk_first, n_used, x_rows, w1, b1r, sel)

    return pl.pallas_call(
        _moe_down_kernel,
        grid_spec=pltpu.PrefetchScalarGridSpec(
            num_scalar_prefetch=3,
            grid=(D_MODEL // MOE_TN2, n_blocks),
            in_specs=[
                pl.BlockSpec((MOE_ROWS, D_EXPERT), row_blk),
                pl.BlockSpec((None, None, D_EXPERT, MOE_TN2),
                             lambda j, i, be, bf, nu: (layer, be[i], 0, j)),
                pl.BlockSpec((None, None, 1, MOE_TN2),
                             lambda j, i, be, bf, nu: (layer, be[i], 0, j)),
            ],
            out_specs=pl.BlockSpec((MOE_ROWS, MOE_TN2), lambda j, i, be, bf, nu: (i, j)),
            scratch_shapes=[pltpu.VMEM((D_EXPERT, MOE_TN2), BF16)],
        ),
        out_shape=jax.ShapeDtypeStruct((n_rows, D_MODEL), F32),
        compiler_params=params,
        name="moe_down",
    )(blk_e, blk_first, n_used, act, w2, b2r)


def _row_copies(idx_ref, base, src_hbm, dst, dst_base, n, sem):
    def copy(r):
        return pltpu.make_async_copy(src_hbm.at[pl.ds(idx_ref[base + r], 1)],
                                     dst.at[pl.ds(dst_base + r, 1)], sem)

    def start(r, carry):
        copy(r).start()
        return carry

    def wait(r, carry):
        copy(r).wait()
        return carry

    lax.fori_loop(0, n, start, 0)
    lax.fori_loop(0, n, wait, 0)


def _gather_rows_kernel(idx_ref, x_hbm, o_ref, buf, sem):
    i = pl.program_id(0)
    _row_copies(idx_ref, i * MOE_ROWS, x_hbm, buf, 0, MOE_ROWS, sem)
    o_ref[...] = buf[...].astype(o_ref.dtype)


def gather_rows(x, idx):
    n_rows = idx.shape[0]
    d = x.shape[1]
    return pl.pallas_call(
        _gather_rows_kernel,
        grid_spec=pltpu.PrefetchScalarGridSpec(
            num_scalar_prefetch=1,
            grid=(n_rows // MOE_ROWS,),
            in_specs=[pl.BlockSpec(memory_space=pl.ANY)],
            out_specs=pl.BlockSpec((MOE_ROWS, d), lambda i, idx_ref: (i, 0)),
            scratch_shapes=[pltpu.VMEM((MOE_ROWS, d), F32), pltpu.SemaphoreType.DMA(())],
        ),
        out_shape=jax.ShapeDtypeStruct((n_rows, d), BF16),
        compiler_params=pltpu.CompilerParams(dimension_semantics=("arbitrary",)),
        name="gather_rows",
    )(idx, x)


COMBINE_TOKENS = 64


def _combine_ln_kernel(dest_ref, x_ref, gate_ref, g_ref, b_ref, y_hbm, o_ref, buf, sem):
    i = pl.program_id(0)
    n = buf.shape[0]
    tb = n // TOP_K
    _row_copies(dest_ref, i * n, y_hbm, buf, 0, n, sem)
    gates = gate_ref[...]
    acc = gates[:, 0:1] * buf[0:tb]
    for k in range(1, TOP_K):
        acc = acc + gates[:, k:k + 1] * buf[k * tb:(k + 1) * tb]
    h = DEEPNORM_ALPHA * x_ref[...] + acc
    hc = h - jnp.mean(h, axis=-1, keepdims=True)
    var = jnp.mean(hc * hc, axis=-1, keepdims=True)
    o_ref[...] = hc * lax.rsqrt(var + LN_EPS) * g_ref[...] + b_ref[...]


def combine_residual_ln(xt, y_rows, dest, gates, g, b):
    n_tok, d = xt.shape
    tb = COMBINE_TOKENS
    dest_km = jnp.transpose(dest.reshape(n_tok // tb, tb, TOP_K), (0, 2, 1)).reshape(-1)
    tok_blk = lambda i, dr: (i, 0)
    prm = pl.BlockSpec((1, d), lambda i, dr: (0, 0))
    return pl.pallas_call(
        _combine_ln_kernel,
        grid_spec=pltpu.PrefetchScalarGridSpec(
            num_scalar_prefetch=1,
            grid=(n_tok // tb,),
            in_specs=[pl.BlockSpec((tb, d), tok_blk), pl.BlockSpec((tb, TOP_K), tok_blk), prm, prm,
                      pl.BlockSpec(memory_space=pl.ANY)],
            out_specs=pl.BlockSpec((tb, d), tok_blk),
            scratch_shapes=[pltpu.VMEM((tb * TOP_K, d), F32), pltpu.SemaphoreType.DMA(())],
        ),
        out_shape=jax.ShapeDtypeStruct((n_tok, d), F32),
        compiler_params=pltpu.CompilerParams(dimension_semantics=("arbitrary",)),
        name="combine_residual_ln",
    )(dest_km, xt, gates, g.astype(F32).reshape(1, d), b.astype(F32).reshape(1, d), y_rows)


def routed_moe_ln(xt, w_router, b_router, w1, b1, w2, b2, layer, ln_g, ln_b):
    n_tok = xt.shape[0]
    logits = (xt @ w_router[layer] + b_router[layer]).astype(F32)
    top_val, top_idx = lax.top_k(logits, TOP_K)
    gates = jax.nn.softmax(top_val, axis=-1)
    tk = n_tok * TOP_K
    flat_e = top_idx.reshape(tk)
    order = jnp.argsort(flat_e)
    sorted_e = flat_e[order]
    counts = jnp.bincount(flat_e, length=N_EXPERTS)
    padded = (counts + MOE_ROWS - 1) // MOE_ROWS * MOE_ROWS
    start = jnp.cumsum(counts) - counts
    pad_end = jnp.cumsum(padded)
    pad_start = pad_end - padded
    dest_sorted = pad_start[sorted_e] + jnp.arange(tk, dtype=jnp.int32) - start[sorted_e]
    dest = jnp.zeros((tk,), jnp.int32).at[order].set(dest_sorted.astype(jnp.int32))
    n_blocks = -(-(tk + N_EXPERTS * (MOE_ROWS - 1)) // MOE_ROWS)
    n_rows = n_blocks * MOE_ROWS
    row_tok = jnp.zeros((n_rows,), jnp.int32).at[dest].set(
        jnp.arange(tk, dtype=jnp.int32) // TOP_K)
    blk_e = jnp.minimum(
        jnp.searchsorted(pad_end, jnp.arange(n_blocks, dtype=jnp.int32) * MOE_ROWS, side='right'),
        N_EXPERTS - 1).astype(jnp.int32)
    blk_first = jnp.concatenate(
        [jnp.ones((1,), jnp.int32), (blk_e[1:] != blk_e[:-1]).astype(jnp.int32)])
    n_used = (pad_end[-1:] // MOE_ROWS).astype(jnp.int32)
    x_rows = gather_rows(xt, row_tok)
    y_rows = _moe_experts(x_rows, blk_e, blk_first, n_used, w1, b1, w2, b2, layer)
    return combine_residual_ln(xt, y_rows, dest.reshape(n_tok, TOP_K), gates, ln_g, ln_b)


def kernel(x_prompt, x_sample, state_rwkv_shift, state_rwkv_wkv, state_gdn_conv, state_gdn_delta, state_ret, ln_gain, ln_bias, rwkv_mu, rwkv_w_rkv, rwkv_w_o, rwkv_w0, rwkv_w1, rwkv_w2, rwkv_a0, rwkv_a1, rwkv_a2, rwkv_g1, rwkv_g2, rwkv_k_k, rwkv_k_a, rwkv_r_k, rwkv_lnx_g, rwkv_lnx_b, rwkv_v0, rwkv_v1, rwkv_v2, gdn_w_in, gdn_conv_w, gdn_a_log, gdn_dt_bias, gdn_norm_w, gdn_w_o, ret_w_in, ret_w_o, moe_w_router, moe_b_router, moe_w1, moe_b1, moe_w2, moe_b2):
    n_p = x_prompt.shape[0]
    n_ptok = x_prompt.shape[0] * x_prompt.shape[1]
    hp, hs = x_prompt, x_sample
    vfirst_p = None
    vfirst_s = None
    p_shift, p_wkv, p_conv, p_delta, p_ret = [], [], [], [], []
    s_shift, s_wkv, s_conv, s_delta, s_ret = [], [], [], [], []
    for i in range(DEPTH):
        kind, j = i % N_MIXERS, i // N_MIXERS
        if kind == 0:
            prm = (rwkv_mu[j], rwkv_w_rkv[j], rwkv_w_o[j], rwkv_w0[j], rwkv_w1[j], rwkv_w2[j],
                   rwkv_a0[j], rwkv_a1[j], rwkv_a2[j], rwkv_g1[j], rwkv_g2[j], rwkv_k_k[j],
                   rwkv_k_a[j], rwkv_r_k[j], rwkv_lnx_g[j], rwkv_lnx_b[j])
            v_res = None if j == 0 else (rwkv_v0[j - 1], rwkv_v1[j - 1], rwkv_v2[j - 1])
            mp, sh_p, st_p, vfirst_p = rwkv7_time_mix_prompt(
                hp, j, vfirst_p, rwkv_mu, rwkv_w_rkv, rwkv_w_o, rwkv_w0, rwkv_w1, rwkv_w2,
                rwkv_a0, rwkv_a1, rwkv_a2, rwkv_g1, rwkv_g2, rwkv_k_k, rwkv_k_a, rwkv_r_k,
                rwkv_lnx_g, rwkv_lnx_b, (rwkv_v0, rwkv_v1, rwkv_v2))
            ms, sh_s, st_s, vfirst_s = rwkv7_time_mix(
                hs, state_rwkv_shift[j], state_rwkv_wkv[j], vfirst_s, v_res, *prm)
            p_shift.append(sh_p)
            p_wkv.append(st_p)
            s_shift.append(sh_s)
            s_wkv.append(st_s)
        elif kind == 1:
            prm = (gdn_w_in[j], gdn_conv_w[j], gdn_a_log[j], gdn_dt_bias[j], gdn_norm_w[j], gdn_w_o[j])
            mp, cv_p, st_p = gated_deltanet_mix(hp, None, None, *prm)
            ms, cv_s, st_s = gated_deltanet_mix(hs, state_gdn_conv[j], state_gdn_delta[j], *prm)
            p_conv.append(cv_p)
            p_delta.append(st_p)
            s_conv.append(cv_s)
            s_delta.append(st_s)
        else:
            mp, st_p = retention_mix(hp, None, 0, ret_w_in[j], ret_w_o[j])
            ms, st_s = retention_mix(hs, state_ret[j], PAST_LEN, ret_w_in[j], ret_w_o[j])
            p_ret.append(st_p)
            s_ret.append(st_s)
        xt = jnp.concatenate([hp.reshape(-1, D_MODEL), hs.reshape(-1, D_MODEL)], axis=0)
        mt = jnp.concatenate([mp.reshape(-1, D_MODEL), ms.reshape(-1, D_MODEL)], axis=0)
        xt = residual_layer_norm(xt, mt, ln_gain[i, 0], ln_bias[i, 0])
        xt = routed_moe_ln(xt, moe_w_router, moe_b_router, moe_w1, moe_b1, moe_w2, moe_b2, i,
                           ln_gain[i, 1], ln_bias[i, 1])
        hp = xt[:n_ptok].reshape(x_prompt.shape)
        hs = xt[n_ptok:].reshape(x_sample.shape)
    stk = lambda rows, like: jnp.stack(rows).astype(like.dtype)
    return (hp, hs,
            stk(p_shift, state_rwkv_shift), stk(p_wkv, state_rwkv_wkv),
            stk(p_conv, state_gdn_conv), stk(p_delta, state_gdn_delta), stk(p_ret, state_ret),
            stk(s_shift, state_rwkv_shift), stk(s_wkv, state_rwkv_wkv),
            stk(s_conv, state_gdn_conv), stk(s_delta, state_gdn_delta), stk(s_ret, state_ret))
```

```python
import functools
import math

import jax
import jax.numpy as jnp
from jax import lax
from jax.experimental import pallas as pl
from jax.experimental.pallas import tpu as pltpu

F32 = jnp.float32
BF16 = jnp.bfloat16

D_MODEL = 4096
DEPTH = 4
PAST_LEN = 16384
N_MIXERS = 3

DEEPNORM_ALPHA = (2.0 * DEPTH) ** 0.25
LN_EPS = 1e-5

A_HEAD = 64
A_HEADS = D_MODEL // A_HEAD
A_GN_EPS = 64e-5

B_DK = 128
B_HEADS = D_MODEL // B_DK
B_DV = 2 * B_DK
B_QK = B_HEADS * B_DK
B_V = B_HEADS * B_DV
B_CONV = 4
B_CONV_CH = 2 * B_QK + B_V
B_CHUNK = 64

C_DK = 256
C_HEADS = D_MODEL // C_DK
C_DV = 2 * C_DK
C_QK = C_HEADS * C_DK
C_V = C_HEADS * C_DV
C_CHUNK = 128
ROT_BASE = 10000.0

N_EXPERTS = 32
TOP_K = 4
D_EXPERT = D_MODEL // 4
SWIGLU_ALPHA = 1.702
SWIGLU_LIMIT = 7.0

LANES = 128

MOE_ROWS = 256
MOE_TN1 = 1024
MOE_TN2 = 2048
LANE_PAIR_GROUP = 256
VMEM_LIMIT = 56 * 1024 * 1024

WKV_CHUNK = 64
WKV_PAIRS = 8
GDN_HEADS = 8
RET_HEADS = 4


def head_norm(x, eps):
    xc = x - x.mean(-1, keepdims=True)
    return xc * lax.rsqrt(jnp.mean(xc * xc, -1, keepdims=True) + eps)


def rms_norm(x, eps):
    return x * lax.rsqrt(jnp.mean(x * x, -1, keepdims=True) + eps)


def l2_normalize(x, eps):
    return x * lax.rsqrt(jnp.sum(x * x, -1, keepdims=True) + eps)


def _dot(a, b):
    return jnp.dot(a.astype(BF16), b.astype(BF16), preferred_element_type=F32)


def _dot_nt(a, b):
    return lax.dot_general(a.astype(BF16), b.astype(BF16), (((1,), (1,)), ((), ())),
                           preferred_element_type=F32)


def _dot_tn(a, b):
    return lax.dot_general(a.astype(BF16), b.astype(BF16), (((0,), (0,)), ((), ())),
                           preferred_element_type=F32)


def _split3(x):
    h1 = x.astype(BF16)
    r1 = x - h1.astype(F32)
    h2 = r1.astype(BF16)
    h3 = (r1 - h2.astype(F32)).astype(BF16)
    return h1, h2, h3


def _cumsum_rows(tril_bf, x):
    h1, h2, h3 = _split3(x)
    d = lambda y: jnp.dot(tril_bf, y, preferred_element_type=F32)
    return d(h1) + (d(h2) + d(h3))


def _level_masks(c):
    row = lax.broadcasted_iota(jnp.int32, (c, c), 0)
    col = lax.broadcasted_iota(jnp.int32, (c, c), 1)
    masks = []
    size = 1
    while size < c:
        same_big = (row // (2 * size)) == (col // (2 * size))
        same_small = (row // size) == (col // size)
        masks.append(same_big & jnp.logical_not(same_small) & (row > col))
        size *= 2
    return masks


def _unit_lower_inverses(a_strict, eye, masks):
    n = range(len(a_strict))
    t = [eye + jnp.where(masks[0], a_strict[i], 0.0) for i in n]
    for m in masks[1:]:
        tn = [_dot(t[i], jnp.where(m, a_strict[i], 0.0)) for i in n]
        t = [t[i] + _dot(tn[i], t[i]) for i in n]
    return t


def _head_sums(x, m0):
    zero = jnp.zeros_like(x)
    s_lo = jnp.sum(jnp.where(m0, x, zero), axis=1, keepdims=True)
    s_hi = jnp.sum(jnp.where(m0, zero, x), axis=1, keepdims=True)
    return jnp.where(m0, s_lo, s_hi)


def _softplus(x):
    return jnp.maximum(x, 0.0) + jnp.log(1.0 + jnp.exp(-jnp.abs(x)))


def _wkv_kernel(*refs, has_vres):
    if has_vres:
        (r_ref, k_ref, v_ref, wp_ref, ap_ref, g_ref, vp_ref, vf_ref,
         kkw_ref, ka_ref, rk_ref, lg_ref, lb_ref, y_ref, s_out_ref, s_scr) = refs
    else:
        (r_ref, k_ref, v_ref, wp_ref, ap_ref, g_ref,
         kkw_ref, ka_ref, rk_ref, lg_ref, lb_ref, y_ref, s_out_ref, s_scr) = refs
    c_idx = pl.program_id(2)
    n_c = pl.num_programs(2)
    c = r_ref.shape[0]
    n_pairs = r_ref.shape[1] // LANES
    pairs = range(n_pairs)
    heads = [(p, h) for p in pairs for h in range(2)]

    @pl.when(c_idx == 0)
    def _():
        s_scr[...] = jnp.zeros_like(s_scr)

    row = lax.broadcasted_iota(jnp.int32, (c, c), 0)
    col = lax.broadcasted_iota(jnp.int32, (c, c), 1)
    incl = row >= col
    strict = row > col
    tril_bf = incl.astype(BF16)
    eye = (row == col).astype(F32)
    lane = lax.broadcasted_iota(jnp.int32, (1, LANES), 1)
    m0 = lane < A_HEAD
    bd = (lax.broadcasted_iota(jnp.int32, (LANES, LANES), 0) // A_HEAD
          == lax.broadcasted_iota(jnp.int32, (LANES, LANES), 1) // A_HEAD)
    masks = _level_masks(c)

    sl = [slice(p * LANES, (p + 1) * LANES) for p in pairs]
    r = [r_ref[:, sl[p]] for p in pairs]
    k_raw = [k_ref[:, sl[p]] for p in pairs]
    v = [v_ref[:, sl[p]] for p in pairs]
    s0 = [s_scr[p] for p in pairs]
    lw = [-jnp.exp(-_softplus(-wp_ref[:, sl[p]]) - 0.5) for p in pairs]
    a = [jax.nn.sigmoid(ap_ref[:, sl[p]]) for p in pairs]
    if has_vres:
        v = [v[p] + (vf_ref[:, sl[p]] - v[p]) * jax.nn.sigmoid(vp_ref[:, sl[p]]) for p in pairs]
    kk_raw = [k_raw[p] * kkw_ref[:, sl[p]] for p in pairs]
    kk = [kk_raw[p] * lax.rsqrt(_head_sums(kk_raw[p] * kk_raw[p], m0) + 1e-12) for p in pairs]
    k = [k_raw[p] * (1.0 + (a[p] - 1.0) * ka_ref[:, sl[p]]) for p in pairs]

    cum = [_cumsum_rows(tril_bf, lw[p]) for p in pairs]
    pw = [jnp.exp(cum[p]) for p in pairs]
    inv_pw = [jnp.exp(-cum[p]) for p in pairs]
    pw_last = [pw[p][c - 1:c, :] for p in pairs]
    at = [-kk[p] * jnp.exp(cum[p] - lw[p]) for p in pairs]
    bt = [kk[p] * a[p] * inv_pw[p] for p in pairs]
    kt = [k[p] * inv_pw[p] for p in pairs]
    rt = [r[p] * pw[p] for p in pairs]

    def gram_lhs(p):
        zero = jnp.zeros_like(at[p])
        return jnp.concatenate([jnp.where(m0, at[p], zero), jnp.where(m0, zero, at[p]),
                                jnp.where(m0, rt[p], zero), jnp.where(m0, zero, rt[p])], axis=0)
    lhs = [gram_lhs(p) for p in pairs]
    g_b = [_dot_nt(lhs[p], bt[p]) for p in pairs]
    g_k = [_dot_nt(lhs[p], kt[p]) for p in pairs]
    a_ab = [jnp.where(strict, g_b[p][h * c:(h + 1) * c], 0.0) for p, h in heads]
    a_ak = [jnp.where(strict, g_k[p][h * c:(h + 1) * c], 0.0) for p, h in heads]
    a_rb = [jnp.where(incl, g_b[p][(2 + h) * c:(3 + h) * c], 0.0) for p, h in heads]
    a_rk = [jnp.where(incl, g_k[p][(2 + h) * c:(3 + h) * c], 0.0) for p, h in heads]

    akv = [_dot(a_ak[i], v[p]) for i, (p, h) in enumerate(heads)]
    y0 = [_dot(a_rk[i], v[p]) for i, (p, h) in enumerate(heads)]
    t = _unit_lower_inverses(a_ab, eye, masks)
    wu = [_dot(t[i], jnp.concatenate([at[p], akv[i]], axis=1)) for i, (p, h) in enumerate(heads)]
    w = [jnp.where(m0, wu[2 * p][:, :LANES], wu[2 * p + 1][:, :LANES]) for p in pairs]
    u0 = [jnp.where(m0, wu[2 * p][:, LANES:], wu[2 * p + 1][:, LANES:]) for p in pairs]

    wr = [_dot_nt(jnp.concatenate([w[p], rt[p]], axis=0), s0[p]) for p in pairs]
    u = [wr[p][:c] + u0[p] for p in pairs]
    aru = [_dot(a_rb[i], u[p]) for i, (p, h) in enumerate(heads)]
    upd = [_dot_tn(jnp.concatenate([u[p], v[p]], axis=0),
                   jnp.concatenate([bt[p] * pw_last[p], kt[p] * pw_last[p]], axis=0)) for p in pairs]
    inv_n = 1.0 / A_HEAD
    for p in pairs:
        y = (wr[p][c:] + jnp.where(m0, y0[2 * p], y0[2 * p + 1])
             + jnp.where(m0, aru[2 * p], aru[2 * p + 1]))
        yc = y - _head_sums(y, m0) * inv_n
        yn = yc * lax.rsqrt(_head_sums(yc * yc, m0) * inv_n + A_GN_EPS)
        yn = yn * lg_ref[:, sl[p]] + lb_ref[:, sl[p]]
        bonus = _head_sums(r[p] * k[p] * rk_ref[:, sl[p]], m0) * v[p]
        y_ref[:, sl[p]] = ((yn + bonus) * g_ref[:, sl[p]]).astype(y_ref.dtype)
        s_scr[p] = s0[p] * pw_last[p] + jnp.where(bd, upd[p], 0.0)

    @pl.when(c_idx == n_c - 1)
    def _():
        for p in pairs:
            s_fin = s_scr[p]
            s_out_ref[2 * p] = s_fin[:A_HEAD, :A_HEAD]
            s_out_ref[2 * p + 1] = s_fin[A_HEAD:, A_HEAD:]


def wkv7_chunked(r, k, v, w_pre, a_pre, g, vres, k_k, k_a, r_k, lnx_g, lnx_b):
    nb, t_len, d = r.shape
    lanes = LANES * WKV_PAIRS
    grid = (nb, d // lanes, t_len // WKV_CHUNK)
    blk = pl.BlockSpec((None, WKV_CHUNK, lanes), lambda b, g, c: (b, c, g))
    prm = pl.BlockSpec((1, lanes), lambda b, g, c: (0, g))
    acts = [r, k, v, w_pre, a_pre, g] + (list(vres) if vres is not None else [])
    prms = [t.astype(F32).reshape(1, d) for t in (k_k, k_a, r_k, lnx_g, lnx_b)]
    return pl.pallas_call(
        functools.partial(_wkv_kernel, has_vres=vres is not None),
        grid=grid,
        in_specs=[blk] * len(acts) + [prm] * len(prms),
        out_specs=[blk, pl.BlockSpec((None, 2 * WKV_PAIRS, A_HEAD, A_HEAD), lambda b, g, c: (b, g, 0, 0))],
        out_shape=[jax.ShapeDtypeStruct((nb, t_len, d), BF16),
                   jax.ShapeDtypeStruct((nb, d // A_HEAD, A_HEAD, A_HEAD), F32)],
        scratch_shapes=[pltpu.VMEM((WKV_PAIRS, LANES, LANES), F32)],
        compiler_params=pltpu.CompilerParams(dimension_semantics=("parallel", "parallel", "arbitrary")),
        name="wkv7_chunked",
    )(*acts, *prms)


MM_TM = 1024
MM_TN = 512
MIX_TM = 512
MIX_TN = 1024
LN_TM = 256


def _mm_kernel(a_ref, w_ref, b_ref, o_ref, wbf_ref, *, act):
    @pl.when(pl.program_id(1) == 0)
    def _():
        wbf_ref[...] = w_ref[...].astype(BF16)

    acc = jnp.dot(a_ref[...].astype(BF16), wbf_ref[...], preferred_element_type=F32) + b_ref[...]
    if act == "tanh":
        acc = jnp.tanh(acc)
    elif act == "sigmoid":
        acc = jax.nn.sigmoid(acc)
    o_ref[...] = acc.astype(o_ref.dtype)


def dense(a, w, w_index=(), bias=None, act=None, out_dtype=F32):
    m, kdim = a.shape
    n = w.shape[-1]
    tm = min(MM_TM, m)
    tn = min(MM_TN, n)
    assert m % tm == 0 and n % tn == 0 and w.shape[-2] == kdim
    lead = tuple(w_index)
    b2 = (jnp.zeros((n,), F32) if bias is None else bias.astype(F32)).reshape(1, n)
    return pl.pallas_call(
        functools.partial(_mm_kernel, act=act),
        grid=(n // tn, m // tm),
        in_specs=[pl.BlockSpec((tm, kdim), lambda j, i: (i, 0)),
                  pl.BlockSpec((None,) * len(lead) + (kdim, tn), lambda j, i: lead + (0, j)),
                  pl.BlockSpec((1, tn), lambda j, i: (0, j))],
        out_specs=pl.BlockSpec((tm, tn), lambda j, i: (i, j)),
        out_shape=jax.ShapeDtypeStruct((m, n), out_dtype),
        scratch_shapes=[pltpu.VMEM((kdim, tn), BF16)],
        compiler_params=pltpu.CompilerParams(dimension_semantics=("arbitrary", "arbitrary"),
                                             vmem_limit_bytes=VMEM_LIMIT),
        name="dense",
    )(a, w, b2)


def _mix_kernel(x_ref, xp_ref, mu_ref, *o_refs):
    x = x_ref[...]
    xx = xp_ref[...] - x
    for j, o_ref in enumerate(o_refs):
        o_ref[...] = (x + xx * mu_ref[j]).astype(o_ref.dtype)


def token_shift_mixes(x, x_prev, mu):
    m, d = x.shape
    n_mix = mu.shape[0]
    blk = pl.BlockSpec((MIX_TM, MIX_TN), lambda i, j: (i, j))
    return pl.pallas_call(
        _mix_kernel,
        grid=(m // MIX_TM, d // MIX_TN),
        in_specs=[blk, blk, pl.BlockSpec((n_mix, 1, MIX_TN), lambda i, j: (0, 0, j))],
        out_specs=[blk] * n_mix,
        out_shape=[jax.ShapeDtypeStruct((m, d), BF16)] * n_mix,
        compiler_params=pltpu.CompilerParams(dimension_semantics=("parallel", "parallel")),
        name="token_shift_mixes",
    )(x, x_prev, mu.astype(F32).reshape(n_mix, 1, d))


def _res_ln_kernel(x_ref, m_ref, g_ref, b_ref, o_ref):
    h = DEEPNORM_ALPHA * x_ref[...] + m_ref[...]
    hc = h - jnp.mean(h, axis=-1, keepdims=True)
    var = jnp.mean(hc * hc, axis=-1, keepdims=True)
    o_ref[...] = hc * lax.rsqrt(var + LN_EPS) * g_ref[...] + b_ref[...]


def residual_layer_norm(x, m, g, b):
    rows, d = x.shape
    blk = pl.BlockSpec((LN_TM, d), lambda i: (i, 0))
    prm = pl.BlockSpec((1, d), lambda i: (0, 0))
    return pl.pallas_call(
        _res_ln_kernel,
        grid=(rows // LN_TM,),
        in_specs=[blk, blk, prm, prm],
        out_specs=blk,
        out_shape=jax.ShapeDtypeStruct((rows, d), F32),
        compiler_params=pltpu.CompilerParams(dimension_semantics=("parallel",),
                                             vmem_limit_bytes=VMEM_LIMIT),
        name="residual_layer_norm",
    )(x, m, g.astype(F32).reshape(1, d), b.astype(F32).reshape(1, d))


def rwkv7_time_mix_prompt(x, j, vfirst, mu, w_rkv, w_o, w0, w1, w2, a0, a1, a2, g1, g2,
                          k_k, k_a, r_k, lnx_g, lnx_b, v012):
    nb, t_len, d = x.shape
    x_prev = jnp.concatenate([jnp.zeros((nb, 1, d), x.dtype), x[:, :-1]], axis=1)
    m = token_shift_mixes(x.reshape(-1, d), x_prev.reshape(-1, d), mu[j])
    r = dense(m[0], w_rkv, (j, 0))
    k = dense(m[1], w_rkv, (j, 1))
    v = dense(m[2], w_rkv, (j, 2))
    w_pre = dense(dense(m[3], w1, (j,), act="tanh", out_dtype=BF16), w2, (j,), bias=w0[j])
    a_pre = dense(dense(m[4], a1, (j,), out_dtype=BF16), a2, (j,), bias=a0[j])
    g = dense(dense(m[5], g1, (j,), act="sigmoid", out_dtype=BF16), g2, (j,))
    b3 = lambda t: t.reshape(nb, t_len, d)
    if vfirst is None:
        vres = None
        vfirst = b3(v)
    else:
        v0, v1, v2 = v012
        vres = (b3(dense(dense(m[2], v1, (j - 1,), out_dtype=BF16), v2, (j - 1,), bias=v0[j - 1])), vfirst)
    y, wkv = wkv7_chunked(b3(r), b3(k), b3(v), b3(w_pre), b3(a_pre), b3(g), vres,
                          k_k[j], k_a[j], r_k[j].reshape(d), lnx_g[j], lnx_b[j])
    out = dense(y.reshape(-1, d), w_o, (j,))
    return b3(out), x[:, -1], wkv, vfirst


def wkv7_short(r, lw, k, v, kk, a, s0):
    t_len = r.shape[1]
    hi = lax.Precision.HIGHEST
    cum = jnp.cumsum(lw, axis=1)
    pw = jnp.exp(cum)
    inv_pw = jnp.exp(-cum)
    pw_last = pw[:, -1]
    at = -kk * jnp.exp(cum - lw)
    bt = kk * a * inv_pw
    kt = k * inv_pw
    rt = r * pw
    idx = jnp.arange(t_len)
    strict = idx[:, None] > idx[None, :]
    incl = idx[:, None] >= idx[None, :]
    gram = lambda x, y, m: jnp.where(m, jnp.einsum('bthn,bshn->bhts', x, y, precision=hi), 0.0)
    a_ab, a_ak = gram(at, bt, strict), gram(at, kt, strict)
    a_rb, a_rk = gram(rt, bt, incl), gram(rt, kt, incl)
    tm = jnp.eye(t_len, dtype=F32) + a_ab
    ap = a_ab
    for _ in range(t_len - 2):
        ap = jnp.einsum('bhts,bhsu->bhtu', ap, a_ab, precision=hi)
        tm = tm + ap
    state_rows = lambda q: jnp.sum(jnp.swapaxes(q, 1, 2)[:, :, :, None, :] * s0[:, :, None], axis=-1)
    x = state_rows(at) + jnp.einsum('bhts,bshi->bhti', a_ak, v, precision=hi)
    u = jnp.einsum('bhts,bhsi->bhti', tm, x, precision=hi)
    y = (state_rows(rt)
         + jnp.einsum('bhts,bhsi->bhti', a_rb, u, precision=hi)
         + jnp.einsum('bhts,bshi->bhti', a_rk, v, precision=hi))
    bl = jnp.swapaxes(bt * pw_last[:, None], 1, 2)
    kl = jnp.swapaxes(kt * pw_last[:, None], 1, 2)
    vh = jnp.swapaxes(v, 1, 2)
    s = s0 * pw_last[:, :, None, :]
    for t in range(t_len):
        s = s + u[:, :, t, :, None] * bl[:, :, t, None, :] + vh[:, :, t, :, None] * kl[:, :, t, None, :]
    return jnp.moveaxis(y, 2, 1), s


def rwkv7_time_mix(x, shift0, wkv0, v_first, v_res, mu, w_rkv, w_o, w0, w1, w2,
                   a0, a1, a2, g1, g2, k_k, k_a, r_k, lnx_g, lnx_b):
    nb, t_len, d = x.shape
    x_prev = jnp.concatenate([shift0[:, None, :].astype(x.dtype), x[:, :-1]], axis=1)
    xx = x_prev - x
    mix = lambda j: x + xx * mu[j]
    r = mix(0) @ w_rkv[0]
    k = mix(1) @ w_rkv[1]
    xv = mix(2)
    v = xv @ w_rkv[2]
    w_log = -jax.nn.softplus(-(w0 + jnp.tanh(mix(3) @ w1) @ w2).astype(F32)) - 0.5
    a = jax.nn.sigmoid((a0 + (mix(4) @ a1) @ a2).astype(F32))
    g = jax.nn.sigmoid(mix(5) @ g1) @ g2
    if v_first is None:
        v_first = v
    else:
        v0, v1, v2 = v_res
        v = v + (v_first - v) * jax.nn.sigmoid(v0 + (xv @ v1) @ v2)
    hs = lambda t: t.astype(F32).reshape(nb, t_len, A_HEADS, A_HEAD)
    r, k, v, a = hs(r), hs(k), hs(v), hs(a)
    kk = l2_normalize(k * k_k.astype(F32).reshape(A_HEADS, A_HEAD), 1e-12)
    k = k * (1.0 + (a - 1.0) * k_a.astype(F32).reshape(A_HEADS, A_HEAD))
    y, wkv = wkv7_short(r, hs(-jnp.exp(w_log)), k, v, kk, a, wkv0.astype(F32))
    y = head_norm(y, A_GN_EPS).reshape(nb, t_len, d) * lnx_g.astype(F32) + lnx_b.astype(F32)
    bonus = jnp.sum(r * k * r_k.astype(F32), -1, keepdims=True) * v
    y = (y + bonus.reshape(nb, t_len, d)) * g.astype(F32)
    return y.astype(x.dtype) @ w_o, x[:, -1], wkv, v_first


def _gdn_kernel(q_ref, k_ref, v_ref, z_ref, beta_ref, g_ref, nw_ref, o_ref, s_out_ref, s_scr):
    c_idx = pl.program_id(2)
    n_c = pl.num_programs(2)
    c = q_ref.shape[0]
    nh = q_ref.shape[1] // B_DK
    hs = range(nh)

    @pl.when(c_idx == 0)
    def _():
        s_scr[...] = jnp.zeros_like(s_scr)

    row = lax.broadcasted_iota(jnp.int32, (c, c), 0)
    col = lax.broadcasted_iota(jnp.int32, (c, c), 1)
    incl = row >= col
    strict = row > col
    tril_bf = incl.astype(BF16)
    triu_bf = (row <= col).astype(BF16)
    eye = (row == col).astype(F32)
    masks = _level_masks(c)

    g = g_ref[...]
    beta = beta_ref[...]
    gc = _cumsum_rows(tril_bf, g)
    g3 = _split3(g)
    tn = lambda x: lax.dot_general(x, triu_bf, (((0,), (0,)), ((), ())), preferred_element_type=F32)
    gc_t = tn(g3[0]) + (tn(g3[1]) + tn(g3[2]))

    q = [q_ref[:, h * B_DK:(h + 1) * B_DK] for h in hs]
    k = [k_ref[:, h * B_DK:(h + 1) * B_DK] for h in hs]
    v = [v_ref[:, h * B_DV:(h + 1) * B_DV] for h in hs]
    z = [z_ref[:, h * B_DV:(h + 1) * B_DV] for h in hs]
    s0 = [s_scr[h] for h in hs]
    nw = nw_ref[...]

    gcol = [gc[:, h:h + 1] for h in hs]
    bcol = [beta[:, h:h + 1] for h in hs]
    glast = [gc[c - 1:c, h:h + 1] for h in hs]
    decay = [jnp.where(incl, jnp.exp(jnp.where(incl, gcol[h] - gc_t[h:h + 1, :], 0.0)), 0.0) for h in hs]
    egc = [jnp.exp(gcol[h]) for h in hs]
    kb = [k[h] * bcol[h] for h in hs]
    gram = [_dot_nt(jnp.concatenate([kb[h], q[h]], axis=0), k[h]) for h in hs]
    a_neg = [jnp.where(strict, -gram[h][:c] * decay[h], 0.0) for h in hs]
    qk = [gram[h][c:] * decay[h] for h in hs]
    t = _unit_lower_inverses(a_neg, eye, masks)
    sol = [_dot(t[h], jnp.concatenate([v[h] * bcol[h], kb[h] * egc[h]], axis=1)) for h in hs]
    wq = [_dot(jnp.concatenate([sol[h][:, B_DV:], q[h] * egc[h]], axis=0), s0[h]) for h in hs]
    v_new = [sol[h][:, :B_DV] - wq[h][:c] for h in hs]
    o = [wq[h][c:] + _dot(qk[h], v_new[h]) for h in hs]
    upd = [_dot_tn(k[h] * jnp.exp(glast[h] - gcol[h]), v_new[h]) for h in hs]
    for h in hs:
        on = o[h] * lax.rsqrt(jnp.mean(o[h] * o[h], axis=-1, keepdims=True) + 1e-6) * nw
        o_ref[:, h * B_DV:(h + 1) * B_DV] = on * (z[h] * jax.nn.sigmoid(z[h]))
        s_scr[h] = s0[h] * jnp.exp(glast[h]) + upd[h]

    @pl.when(c_idx == n_c - 1)
    def _():
        s_out_ref[...] = s_scr[...]


def gdn_chunked(q, k, v, proj, beta_g, g_g, norm_w):
    nb, t_len, _ = q.shape
    n_groups = B_HEADS // GDN_HEADS
    grid = (nb, n_groups, t_len // B_CHUNK)
    v_width = GDN_HEADS * B_DV
    z_first = B_CONV_CH // v_width
    qk_blk = pl.BlockSpec((None, B_CHUNK, GDN_HEADS * B_DK), lambda b, g, c: (b, c, g))
    v_blk = pl.BlockSpec((None, B_CHUNK, v_width), lambda b, g, c: (b, c, g))
    z_blk = pl.BlockSpec((None, B_CHUNK, v_width), lambda b, g, c: (b, c, z_first + g))
    sc_blk = pl.BlockSpec((None, None, B_CHUNK, GDN_HEADS), lambda b, g, c: (b, g, c, 0))
    return pl.pallas_call(
        _gdn_kernel,
        grid=grid,
        in_specs=[qk_blk, qk_blk, v_blk, z_blk, sc_blk, sc_blk,
                  pl.BlockSpec((1, B_DV), lambda b, g, c: (0, 0))],
        out_specs=[v_blk, pl.BlockSpec((None, GDN_HEADS, B_DK, B_DV), lambda b, g, c: (b, g, 0, 0))],
        out_shape=[jax.ShapeDtypeStruct((nb, t_len, B_V), F32),
                   jax.ShapeDtypeStruct((nb, B_HEADS, B_DK, B_DV), F32)],
        scratch_shapes=[pltpu.VMEM((GDN_HEADS, B_DK, B_DV), F32)],
        compiler_params=pltpu.CompilerParams(dimension_semantics=("parallel", "parallel", "arbitrary")),
        name="gdn_chunked",
    )(q, k, v, proj, beta_g, g_g, norm_w.reshape(1, B_DV))


def gated_delta_chunked(q, k, v, beta, g, s0):
    nb, nh, t_len, dk = q.shape
    dv = v.shape[-1]
    c = math.gcd(t_len, B_CHUNK)
    n = t_len // c
    q, k, v = (t.reshape(nb, nh, n, c, t.shape[-1]) for t in (q, k, v))
    beta, g = (t.reshape(nb, nh, n, c) for t in (beta, g))
    gc = jnp.cumsum(g, axis=-1)
    causal = jnp.tril(jnp.ones((c, c), dtype=bool))
    eye = jnp.eye(c, dtype=jnp.float32)
    diff = gc[..., :, None] - gc[..., None, :]
    decay = jnp.where(causal, jnp.exp(jnp.where(causal, diff, 0.0)), 0.0)
    kb = k * beta[..., None]
    lower = jnp.einsum('bhncd,bhnsd->bhncs', kb, k) * decay * (1.0 - eye)
    rhs = jnp.concatenate([v * beta[..., None], kb * jnp.exp(gc)[..., None]], axis=-1)
    sol = lax.linalg.triangular_solve(lower + eye, rhs, left_side=True, lower=True)
    u, w = sol[..., :dv], sol[..., dv:]
    qk = jnp.einsum('bhncd,bhnsd->bhncs', q, k) * decay
    q_dec = q * jnp.exp(gc)[..., None]
    k_dec = k * jnp.exp(gc[..., -1:] - gc)[..., None]
    g_tot = jnp.exp(gc[..., -1])[..., None, None]

    def step(s, inp):
        u_c, w_c, qk_c, qd_c, kd_c, gt_c = inp
        v_new = u_c - jnp.einsum('bhcd,bhde->bhce', w_c, s)
        o = jnp.einsum('bhcd,bhde->bhce', qd_c, s) + jnp.einsum('bhcs,bhse->bhce', qk_c, v_new)
        s = s * gt_c + jnp.einsum('bhcd,bhce->bhde', kd_c, v_new)
        return s, o
    xs = tuple(jnp.moveaxis(t, 2, 0) for t in (u, w, qk, q_dec, k_dec, g_tot))
    s, o = lax.scan(step, s0, xs)
    return jnp.moveaxis(o, 0, 2).reshape(nb, nh, t_len, dv), s


def gated_deltanet_mix(x, conv0, s0, w_in, conv_w, a_log, dt_bias, norm_w, w_o):
    nb, t_len, _ = x.shape
    proj = x @ w_in
    qkv = proj[..., :B_CONV_CH]
    a_in = proj[..., B_CONV_CH + B_V:B_CONV_CH + B_V + B_HEADS]
    b_in = proj[..., B_CONV_CH + B_V + B_HEADS:]
    if conv0 is None:
        conv0 = jnp.zeros((nb, B_CONV - 1, B_CONV_CH), qkv.dtype)
    xp = jnp.concatenate([conv0.astype(qkv.dtype), qkv], axis=1)
    conv = xp[:, 0:t_len] * conv_w[0]
    for j in range(1, B_CONV):
        conv = conv + xp[:, j:j + t_len] * conv_w[j]
    conv = jax.nn.silu(conv.astype(F32))
    new_conv = xp[:, t_len:]
    q = conv[..., :B_QK].reshape(nb, t_len, B_HEADS, B_DK)
    k = conv[..., B_QK:2 * B_QK].reshape(nb, t_len, B_HEADS, B_DK)
    v = conv[..., 2 * B_QK:].reshape(nb, t_len, B_HEADS, B_DV)
    q = l2_normalize(q, 1e-6) * (B_DK ** -0.5)
    k = l2_normalize(k, 1e-6)
    beta = jax.nn.sigmoid(b_in.astype(F32))
    g = -jnp.exp(a_log.astype(F32)) * jax.nn.softplus(a_in.astype(F32) + dt_bias.astype(F32))
    if s0 is None:
        grp = lambda t: jnp.transpose(t.reshape(nb, t_len, B_HEADS // GDN_HEADS, GDN_HEADS), (0, 2, 1, 3))
        o, s = gdn_chunked(q.reshape(nb, t_len, B_QK), k.reshape(nb, t_len, B_QK),
                           v.reshape(nb, t_len, B_V), proj, grp(beta), grp(g), norm_w.astype(F32))
        return o.astype(x.dtype) @ w_o, new_conv, s
    z = proj[..., B_CONV_CH:B_CONV_CH + B_V]
    tr = lambda t: jnp.swapaxes(t, 1, 2)
    o, s = gated_delta_chunked(tr(q), tr(k), tr(v), tr(beta), tr(g), s0.astype(F32))
    o = rms_norm(tr(o), 1e-6) * norm_w.astype(F32)
    o = o * jax.nn.silu(z.astype(F32).reshape(nb, t_len, B_HEADS, B_DV))
    return o.reshape(nb, t_len, B_V).astype(x.dtype) @ w_o, new_conv, s


def _ret_kernel(lg_ref, q_ref, k_ref, v_ref, gate_ref, cos_ref, sin_ref, o_ref, s_out_ref, s_scr):
    g_idx = pl.program_id(1)
    c_idx = pl.program_id(2)
    n_c = pl.num_programs(2)
    c = q_ref.shape[0]
    nh = q_ref.shape[1] // C_DK
    hs = range(nh)

    @pl.when(c_idx == 0)
    def _():
        s_scr[...] = jnp.zeros_like(s_scr)

    width = q_ref.shape[1]
    lane = lax.broadcasted_iota(jnp.int32, (1, width), 1)
    even = (lane % 2) == 0
    cos = jnp.concatenate([cos_ref[...]] * nh, axis=1)
    sin = jnp.concatenate([sin_ref[...]] * nh, axis=1)

    def rotate(x):
        partner = jnp.where(even, pltpu.roll(x, width - 1, axis=1), pltpu.roll(x, 1, axis=1))
        return x * cos + partner * sin

    qr = rotate(q_ref[...])
    kr = rotate(k_ref[...]) * (C_DK ** -0.5)
    row = lax.broadcasted_iota(jnp.int32, (c, c), 0)
    col = lax.broadcasted_iota(jnp.int32, (c, c), 1)
    rel = (row - col).astype(F32)
    causal = row >= col
    idx = lax.broadcasted_iota(jnp.int32, (c, 1), 0).astype(F32)

    lg = [lg_ref[g_idx * nh + h] for h in hs]
    q = [qr[:, h * C_DK:(h + 1) * C_DK] for h in hs]
    k = [kr[:, h * C_DK:(h + 1) * C_DK] for h in hs]
    v = [v_ref[:, h * C_DV:(h + 1) * C_DV] for h in hs]
    s0 = [s_scr[h] for h in hs]
    dmat = [jnp.where(causal, jnp.exp(jnp.where(causal, rel, 0.0) * lg[h]), 0.0) for h in hs]
    inner = [_dot_nt(q[h], k[h]) * dmat[h] for h in hs]
    o = [_dot(inner[h], v[h]) + _dot(q[h] * jnp.exp((idx + 1.0) * lg[h]), s0[h]) for h in hs]
    upd = [_dot_tn(k[h] * jnp.exp((c - 1.0 - idx) * lg[h]), v[h]) for h in hs]
    for h in hs:
        oc = o[h] - jnp.mean(o[h], axis=-1, keepdims=True)
        on = oc * lax.rsqrt(jnp.mean(oc * oc, axis=-1, keepdims=True) + 1e-5)
        gt = gate_ref[:, h * C_DV:(h + 1) * C_DV]
        o_ref[:, h * C_DV:(h + 1) * C_DV] = gt * jax.nn.sigmoid(gt) * on
        s_scr[h] = s0[h] * jnp.exp(c * lg[h]) + upd[h]

    @pl.when(c_idx == n_c - 1)
    def _():
        s_out_ref[...] = s_scr[...]


def retention_chunked_pallas(proj, cos, sin_signed, log_gamma):
    nb, t_len, _ = proj.shape
    n_groups = C_HEADS // RET_HEADS
    grid = (nb, n_groups, t_len // C_CHUNK)
    qk_width = RET_HEADS * C_DK
    v_width = RET_HEADS * C_DV
    k_first = C_QK // qk_width
    v_first = 2 * C_QK // v_width
    gate_first = (2 * C_QK + C_V) // v_width
    q_blk = pl.BlockSpec((None, C_CHUNK, qk_width), lambda b, g, c, lg: (b, c, g))
    k_blk = pl.BlockSpec((None, C_CHUNK, qk_width), lambda b, g, c, lg: (b, c, k_first + g))
    v_blk = pl.BlockSpec((None, C_CHUNK, v_width), lambda b, g, c, lg: (b, c, v_first + g))
    gate_blk = pl.BlockSpec((None, C_CHUNK, v_width), lambda b, g, c, lg: (b, c, gate_first + g))
    o_blk = pl.BlockSpec((None, C_CHUNK, v_width), lambda b, g, c, lg: (b, c, g))
    tab_blk = pl.BlockSpec((C_CHUNK, C_DK), lambda b, g, c, lg: (c, 0))
    return pl.pallas_call(
        _ret_kernel,
        grid_spec=pltpu.PrefetchScalarGridSpec(
            num_scalar_prefetch=1,
            grid=grid,
            in_specs=[q_blk, k_blk, v_blk, gate_blk, tab_blk, tab_blk],
            out_specs=[o_blk, pl.BlockSpec((None, RET_HEADS, C_DK, C_DV), lambda b, g, c, lg: (b, g, 0, 0))],
            scratch_shapes=[pltpu.VMEM((RET_HEADS, C_DK, C_DV), F32)],
        ),
        out_shape=[jax.ShapeDtypeStruct((nb, t_len, C_V), F32),
                   jax.ShapeDtypeStruct((nb, C_HEADS, C_DK, C_DV), F32)],
        compiler_params=pltpu.CompilerParams(dimension_semantics=("parallel", "parallel", "arbitrary"),
                                             vmem_limit_bytes=VMEM_LIMIT),
        name="retention_chunked",
    )(log_gamma, proj, proj, proj, proj, cos, sin_signed)


def rotary_tables(t_len, start):
    half = C_DK // 2
    inv_freq = 1.0 / (ROT_BASE ** jnp.linspace(0.0, 1.0, half, dtype=jnp.float32))
    pos = start + jnp.arange(t_len, dtype=F32)
    ang = pos[:, None] * inv_freq[None, :]
    sin, cos = jnp.sin(ang), jnp.cos(ang)
    return jnp.repeat(cos, 2, axis=1), jnp.stack([-sin, sin], axis=-1).reshape(t_len, C_DK)


def rel_rotate(x, pos):
    half = x.shape[-1] // 2
    inv_freq = 1.0 / (ROT_BASE ** jnp.linspace(0.0, 1.0, half, dtype=jnp.float32))
    ang = pos[:, None] * inv_freq[None, :]
    sin, cos = jnp.sin(ang), jnp.cos(ang)
    x1, x2 = x[..., 0::2], x[..., 1::2]
    return jnp.stack([x1 * cos - x2 * sin, x2 * cos + x1 * sin], axis=-1).reshape(x.shape)


def retention_chunked(q, k, v, log_gamma, s0):
    nb, nh, t_len, _ = q.shape
    dv = v.shape[-1]
    c = math.gcd(t_len, C_CHUNK)
    n = t_len // c
    q, k, v = (t.reshape(nb, nh, n, c, t.shape[-1]) for t in (q, k, v))
    idx = jnp.arange(c, dtype=jnp.float32)
    lg = log_gamma[:, None]
    rel = idx[:, None] - idx[None, :]
    causal = rel >= 0
    dmat = jnp.where(causal, jnp.exp(jnp.where(causal, rel, 0.0)[None] * lg[..., None]), 0.0)
    inner = jnp.einsum('bhncd,bhnsd->bhncs', q, k) * dmat[:, None]
    o_inner = jnp.einsum('bhncs,bhnse->bhnce', inner, v)
    q_dec = q * jnp.exp((idx + 1.0) * lg)[:, None, :, None]
    k_dec = k * jnp.exp((c - 1.0 - idx) * lg)[:, None, :, None]
    chunk_decay = jnp.exp(c * log_gamma)[:, None, None]

    def step(s, inp):
        qd, kd, vc, oi = inp
        o = oi + jnp.einsum('bhcd,bhde->bhce', qd, s)
        s = s * chunk_decay + jnp.einsum('bhcd,bhce->bhde', kd, vc)
        return s, o
    xs = tuple(jnp.moveaxis(t, 2, 0) for t in (q_dec, k_dec, v, o_inner))
    s, o = lax.scan(step, s0, xs)
    return jnp.moveaxis(o, 0, 2).reshape(nb, nh, t_len, dv), s


def retention_mix(x, s0, start, w_in, w_o):
    nb, t_len, _ = x.shape
    proj = x @ w_in
    log_gamma = jnp.log(1.0 - 2.0 ** (-5.0 - jnp.arange(C_HEADS, dtype=F32)))
    if s0 is None:
        cos, sin_signed = rotary_tables(t_len, start)
        o, s = retention_chunked_pallas(proj.astype(F32), cos, sin_signed, log_gamma)
        return o.astype(x.dtype) @ w_o, s
    heads = lambda t, hd: jnp.swapaxes(t.astype(F32).reshape(nb, t_len, C_HEADS, hd), 1, 2)
    q = heads(proj[..., :C_QK], C_DK)
    k = heads(proj[..., C_QK:2 * C_QK], C_DK)
    v = heads(proj[..., 2 * C_QK:2 * C_QK + C_V], C_DV)
    gate = proj[..., 2 * C_QK + C_V:]
    pos = start + jnp.arange(t_len, dtype=F32)
    q = rel_rotate(q, pos)
    k = rel_rotate(k, pos) * (C_DK ** -0.5)
    o, s = retention_chunked(q, k, v, log_gamma, s0.astype(F32))
    o = head_norm(jnp.swapaxes(o, 1, 2), 1e-5).reshape(nb, t_len, C_V)
    return (jax.nn.silu(gate.astype(F32)) * o).astype(x.dtype) @ w_o, s


def _moe_up_kernel(blk_e_ref, blk_first_ref, n_used_ref, x_ref, w1_ref, b1_ref, sel_ref,
                   o_ref, wbf_ref):
    del blk_e_ref
    i = pl.program_id(1)
    tn = w1_ref.shape[-1]

    @pl.when(i >= n_used_ref[0])
    def _():
        o_ref[...] = jnp.zeros_like(o_ref)

    @pl.when(i < n_used_ref[0])
    def _():
        @pl.when(blk_first_ref[i] == 1)
        def _():
            wbf_ref[...] = w1_ref[...].astype(BF16)

        h = jnp.dot(x_ref[...], wbf_ref[...], preferred_element_type=F32) + b1_ref[...]
        glu = jnp.minimum(h, SWIGLU_LIMIT)
        lin = jnp.clip(pltpu.roll(h, tn - 1, axis=1), -SWIGLU_LIMIT, SWIGLU_LIMIT)
        act = (glu * jax.nn.sigmoid(SWIGLU_ALPHA * glu) * (lin + 1.0)).astype(BF16)
        half = LANE_PAIR_GROUP // 2
        for c in range(tn // LANE_PAIR_GROUP):
            grp = act[:, c * LANE_PAIR_GROUP:(c + 1) * LANE_PAIR_GROUP]
            o_ref[:, c * half:(c + 1) * half] = jnp.dot(
                grp, sel_ref[...], preferred_element_type=F32).astype(o_ref.dtype)


def _moe_down_kernel(blk_e_ref, blk_first_ref, n_used_ref, h_ref, w2_ref, b2_ref, o_ref, wbf_ref):
    del blk_e_ref
    i = pl.program_id(1)

    @pl.when(i >= n_used_ref[0])
    def _():
        o_ref[...] = jnp.zeros_like(o_ref)

    @pl.when(i < n_used_ref[0])
    def _():
        @pl.when(blk_first_ref[i] == 1)
        def _():
            wbf_ref[...] = w2_ref[...].astype(BF16)

        o_ref[...] = jnp.dot(h_ref[...], wbf_ref[...], preferred_element_type=F32) + b2_ref[...]


def _moe_experts(x_rows, blk_e, blk_first, n_used, w1, b1, w2, b2, layer):
    n_rows = x_rows.shape[0]
    n_blocks = n_rows // MOE_ROWS
    d_up = 2 * D_EXPERT
    sel = (jnp.arange(LANE_PAIR_GROUP)[:, None] == 2 * jnp.arange(LANE_PAIR_GROUP // 2)[None, :]).astype(BF16)
    b1r = b1.reshape(DEPTH, N_EXPERTS, 1, d_up)
    b2r = b2.reshape(DEPTH, N_EXPERTS, 1, D_MODEL)
    row_blk = lambda j, i, be, bf, nu: (jnp.minimum(i, nu[0] - 1), 0)
    params = pltpu.CompilerParams(dimension_semantics=("arbitrary", "arbitrary"),
                                  vmem_limit_bytes=VMEM_LIMIT)

    act = pl.pallas_call(
        _moe_up_kernel,
        grid_spec=pltpu.PrefetchScalarGridSpec(
            num_scalar_prefetch=3,
            grid=(d_up // MOE_TN1, n_blocks),
            in_specs=[
                pl.BlockSpec((MOE_ROWS, D_MODEL), row_blk),
                pl.BlockSpec((None, None, D_MODEL, MOE_TN1),
                             lambda j, i, be, bf, nu: (layer, be[i], 0, j)),
                pl.BlockSpec((None, None, 1, MOE_TN1),
                             lambda j, i, be, bf, nu: (layer, be[i], 0, j)),
                pl.BlockSpec((LANE_PAIR_GROUP, LANE_PAIR_GROUP // 2),
                             lambda j, i, be, bf, nu: (0, 0)),
            ],
            out_specs=pl.BlockSpec((MOE_ROWS, MOE_TN1 // 2), lambda j, i, be, bf, nu: (i, j)),
            scratch_shapes=[pltpu.VMEM((D_MODEL, MOE_TN1), BF16)],
        ),
        out_shape=jax.ShapeDtypeStruct((n_rows, D_EXPERT), BF16),
        compiler_params=params,
        name="moe_up",
    )(blk_e, blk_first, n_used, x_rows, w1, b1r, sel)

    return pl.pallas_call(
        _moe_down_kernel,
        grid_spec=pltpu.PrefetchScalarGridSpec(
            num_scalar_prefetch=3,
            grid=(D_MODEL // MOE_TN2, n_blocks),
            in_specs=[
                pl.BlockSpec((MOE_ROWS, D_EXPERT), row_blk),
                pl.BlockSpec((None, None, D_EXPERT, MOE_TN2),
                             lambda j, i, be, bf, nu: (layer, be[i], 0, j)),
                pl.BlockSpec((None, None, 1, MOE_TN2),
                             lambda j, i, be, bf, nu: (layer, be[i], 0, j)),
            ],
            out_specs=pl.BlockSpec((MOE_ROWS, MOE_TN2), lambda j, i, be, bf, nu: (i, j)),
            scratch_shapes=[pltpu.VMEM((D_EXPERT, MOE_TN2), BF16)],
        ),
        out_shape=jax.ShapeDtypeStruct((n_rows, D_MODEL), F32),
        compiler_params=params,
        name="moe_down",
    )(blk_e, blk_first, n_used, act, w2, b2r)


ROW_COPY_UNROLL = 8


def _row_copies(idx_ref, src_hbm, buf, sem, step, slot, wait):
    n = buf.shape[1]

    def body(r, carry):
        cp = pltpu.make_async_copy(src_hbm.at[pl.ds(idx_ref[step * n + r], 1)],
                                   buf.at[slot, pl.ds(r, 1)], sem.at[slot])
        if wait:
            cp.wait()
        else:
            cp.start()
        return carry

    lax.fori_loop(0, n, body, 0, unroll=ROW_COPY_UNROLL)


def _prefetched_rows(idx_ref, src_hbm, buf, sem):
    i = pl.program_id(0)
    slot = i % 2

    @pl.when(i == 0)
    def _():
        _row_copies(idx_ref, src_hbm, buf, sem, 0, 0, wait=False)

    @pl.when(i + 1 < pl.num_programs(0))
    def _():
        _row_copies(idx_ref, src_hbm, buf, sem, i + 1, 1 - slot, wait=False)

    _row_copies(idx_ref, src_hbm, buf, sem, i, slot, wait=True)
    return slot


def _gather_rows_kernel(idx_ref, x_hbm, o_ref, buf, sem):
    slot = _prefetched_rows(idx_ref, x_hbm, buf, sem)
    o_ref[...] = buf[slot].astype(o_ref.dtype)


def gather_rows(x, idx):
    n_rows = idx.shape[0]
    d = x.shape[1]
    return pl.pallas_call(
        _gather_rows_kernel,
        grid_spec=pltpu.PrefetchScalarGridSpec(
            num_scalar_prefetch=1,
            grid=(n_rows // MOE_ROWS,),
            in_specs=[pl.BlockSpec(memory_space=pl.ANY)],
            out_specs=pl.BlockSpec((MOE_ROWS, d), lambda i, idx_ref: (i, 0)),
            scratch_shapes=[pltpu.VMEM((2, MOE_ROWS, d), F32), pltpu.SemaphoreType.DMA((2,))],
        ),
        out_shape=jax.ShapeDtypeStruct((n_rows, d), BF16),
        compiler_params=pltpu.CompilerParams(dimension_semantics=("arbitrary",)),
        name="gather_rows",
    )(idx, x)


COMBINE_TOKENS = 64


def _combine_ln_kernel(dest_ref, x_ref, gate_ref, g_ref, b_ref, y_hbm, o_ref, buf, sem):
    tb = buf.shape[1] // TOP_K
    slot = _prefetched_rows(dest_ref, y_hbm, buf, sem)
    gates = gate_ref[...]
    acc = gates[:, 0:1] * buf[slot, 0:tb]
    for k in range(1, TOP_K):
        acc = acc + gates[:, k:k + 1] * buf[slot, k * tb:(k + 1) * tb]
    h = DEEPNORM_ALPHA * x_ref[...] + acc
    hc = h - jnp.mean(h, axis=-1, keepdims=True)
    var = jnp.mean(hc * hc, axis=-1, keepdims=True)
    o_ref[...] = hc * lax.rsqrt(var + LN_EPS) * g_ref[...] + b_ref[...]


def combine_residual_ln(xt, y_rows, dest, gates, g, b):
    n_tok, d = xt.shape
    tb = COMBINE_TOKENS
    dest_km = jnp.transpose(dest.reshape(n_tok // tb, tb, TOP_K), (0, 2, 1)).reshape(-1)
    tok_blk = lambda i, dr: (i, 0)
    prm = pl.BlockSpec((1, d), lambda i, dr: (0, 0))
    return pl.pallas_call(
        _combine_ln_kernel,
        grid_spec=pltpu.PrefetchScalarGridSpec(
            num_scalar_prefetch=1,
            grid=(n_tok // tb,),
            in_specs=[pl.BlockSpec((tb, d), tok_blk), pl.BlockSpec((tb, TOP_K), tok_blk), prm, prm,
                      pl.BlockSpec(memory_space=pl.ANY)],
            out_specs=pl.BlockSpec((tb, d), tok_blk),
            scratch_shapes=[pltpu.VMEM((2, tb * TOP_K, d), F32), pltpu.SemaphoreType.DMA((2,))],
        ),
        out_shape=jax.ShapeDtypeStruct((n_tok, d), F32),
        compiler_params=pltpu.CompilerParams(dimension_semantics=("arbitrary",)),
        name="combine_residual_ln",
    )(dest_km, xt, gates, g.astype(F32).reshape(1, d), b.astype(F32).reshape(1, d), y_rows)


def routed_moe_ln(xt, w_router, b_router, w1, b1, w2, b2, layer, ln_g, ln_b):
    n_tok = xt.shape[0]
    logits = (xt @ w_router[layer] + b_router[layer]).astype(F32)
    top_val, top_idx = lax.top_k(logits, TOP_K)
    gates = jax.nn.softmax(top_val, axis=-1)
    tk = n_tok * TOP_K
    flat_e = top_idx.reshape(tk)
    order = jnp.argsort(flat_e)
    sorted_e = flat_e[order]
    counts = jnp.bincount(flat_e, length=N_EXPERTS)
    padded = (counts + MOE_ROWS - 1) // MOE_ROWS * MOE_ROWS
    start = jnp.cumsum(counts) - counts
    pad_end = jnp.cumsum(padded)
    pad_start = pad_end - padded
    dest_sorted = pad_start[sorted_e] + jnp.arange(tk, dtype=jnp.int32) - start[sorted_e]
    dest = jnp.zeros((tk,), jnp.int32).at[order].set(dest_sorted.astype(jnp.int32))
    n_blocks = -(-(tk + N_EXPERTS * (MOE_ROWS - 1)) // MOE_ROWS)
    n_rows = n_blocks * MOE_ROWS
    row_tok = jnp.zeros((n_rows,), jnp.int32).at[dest].set(
        jnp.arange(tk, dtype=jnp.int32) // TOP_K)
    blk_e = jnp.minimum(
        jnp.searchsorted(pad_end, jnp.arange(n_blocks, dtype=jnp.int32) * MOE_ROWS, side='right'),
        N_EXPERTS - 1).astype(jnp.int32)
    blk_first = jnp.concatenate(
        [jnp.ones((1,), jnp.int32), (blk_e[1:] != blk_e[:-1]).astype(jnp.int32)])
    n_used = (pad_end[-1:] // MOE_ROWS).astype(jnp.int32)
    x_rows = gather_rows(xt, row_tok)
    y_rows = _moe_experts(x_rows, blk_e, blk_first, n_used, w1, b1, w2, b2, layer)
    return combine_residual_ln(xt, y_rows, dest.reshape(n_tok, TOP_K), gates, ln_g, ln_b)


def kernel(x_prompt, x_sample, state_rwkv_shift, state_rwkv_wkv, state_gdn_conv, state_gdn_delta, state_ret, ln_gain, ln_bias, rwkv_mu, rwkv_w_rkv, rwkv_w_o, rwkv_w0, rwkv_w1, rwkv_w2, rwkv_a0, rwkv_a1, rwkv_a2, rwkv_g1, rwkv_g2, rwkv_k_k, rwkv_k_a, rwkv_r_k, rwkv_lnx_g, rwkv_lnx_b, rwkv_v0, rwkv_v1, rwkv_v2, gdn_w_in, gdn_conv_w, gdn_a_log, gdn_dt_bias, gdn_norm_w, gdn_w_o, ret_w_in, ret_w_o, moe_w_router, moe_b_router, moe_w1, moe_b1, moe_w2, moe_b2):
    n_ptok =x_prompt.shape[0] * x_prompt.shape[1]
    hp, hs = x_prompt, x_sample
    vfirst_p = None
    vfirst_s = None
    p_shift, p_wkv, p_conv, p_delta, p_ret = [], [], [], [], []
    s_shift, s_wkv, s_conv, s_delta, s_ret = [], [], [], [], []
    for i in range(DEPTH):
        kind, j = i % N_MIXERS, i // N_MIXERS
        if kind == 0:
            prm = (rwkv_mu[j], rwkv_w_rkv[j], rwkv_w_o[j], rwkv_w0[j], rwkv_w1[j], rwkv_w2[j],
                   rwkv_a0[j], rwkv_a1[j], rwkv_a2[j], rwkv_g1[j], rwkv_g2[j], rwkv_k_k[j],
                   rwkv_k_a[j], rwkv_r_k[j], rwkv_lnx_g[j], rwkv_lnx_b[j])
            v_res = None if j == 0 else (rwkv_v0[j - 1], rwkv_v1[j - 1], rwkv_v2[j - 1])
            mp, sh_p, st_p, vfirst_p = rwkv7_time_mix_prompt(
                hp, j, vfirst_p, rwkv_mu, rwkv_w_rkv, rwkv_w_o, rwkv_w0, rwkv_w1, rwkv_w2,
                rwkv_a0, rwkv_a1, rwkv_a2, rwkv_g1, rwkv_g2, rwkv_k_k, rwkv_k_a, rwkv_r_k,
                rwkv_lnx_g, rwkv_lnx_b, (rwkv_v0, rwkv_v1, rwkv_v2))
            ms, sh_s, st_s, vfirst_s = rwkv7_time_mix(
                hs, state_rwkv_shift[j], state_rwkv_wkv[j], vfirst_s, v_res, *prm)
            p_shift.append(sh_p)
            p_wkv.append(st_p)
            s_shift.append(sh_s)
            s_wkv.append(st_s)
        elif kind == 1:
            prm = (gdn_w_in[j], gdn_conv_w[j], gdn_a_log[j], gdn_dt_bias[j], gdn_norm_w[j], gdn_w_o[j])
            mp, cv_p, st_p = gated_deltanet_mix(hp, None, None, *prm)
            ms, cv_s, st_s = gated_deltanet_mix(hs, state_gdn_conv[j], state_gdn_delta[j], *prm)
            p_conv.append(cv_p)
            p_delta.append(st_p)
            s_conv.append(cv_s)
            s_delta.append(st_s)
        else:
            mp, st_p = retention_mix(hp, None, 0, ret_w_in[j], ret_w_o[j])
            ms, st_s = retention_mix(hs, state_ret[j], PAST_LEN, ret_w_in[j], ret_w_o[j])
            p_ret.append(st_p)
            s_ret.append(st_s)
        xt = jnp.concatenate([hp.reshape(-1, D_MODEL), hs.reshape(-1, D_MODEL)], axis=0)
        mt = jnp.concatenate([mp.reshape(-1, D_MODEL), ms.reshape(-1, D_MODEL)], axis=0)
        xt = residual_layer_norm(xt, mt, ln_gain[i, 0], ln_bias[i, 0])
        xt = routed_moe_ln(xt, moe_w_router, moe_b_router, moe_w1, moe_b1, moe_w2, moe_b2, i,
                           ln_gain[i, 1], ln_bias[i, 1])
        hp = xt[:n_ptok].reshape(x_prompt.shape)
        hs = xt[n_ptok:].reshape(x_sample.shape)
    stk = lambda rows, like: jnp.stack(rows).astype(like.dtype)
    return (hp, hs,
            stk(p_shift, state_rwkv_shift), stk(p_wkv, state_rwkv_wkv),
            stk(p_conv, state_gdn_conv), stk(p_delta, state_gdn_delta), stk(p_ret, state_ret),
            stk(s_shift, state_rwkv_shift), stk(s_wkv, state_rwkv_wkv),
            stk(s_conv, state_gdn_conv), stk(s_delta, state_gdn_delta), stk(s_ret, state_ret))
```
